```python
import math
import jax, jax.numpy as jnp
from jax import lax
import numpy as np

D_MODEL = 4096
BATCH = 1
SEQ = 16384
DEPTH = 1

N_META = 16
GRID_W = 64
Q_BLOCK = 128
HEAD_DIM = 128
N_Q_HEADS = 16
N_KV_HEADS = 4
GQA_GROUP = N_Q_HEADS // N_KV_HEADS
ATTN_WIDTH = N_Q_HEADS * HEAD_DIM
KV_WIDTH = N_KV_HEADS * HEAD_DIM
ROPE_THETA = 10000.0
ROPE_AXIS_DIM = HEAD_DIM // 2
ROPE_HALF = ROPE_AXIS_DIM // 2
HYENA_WIDTH = D_MODEL // 2
HYENA_ORDER = 2
HYENA_SHORT_CONV = 3
FILTER_EMB_DIM = 33
FILTER_BANDS = (FILTER_EMB_DIM - 1) // 2
FILTER_HIDDEN = 64
FILTER_OUT_SCALE = 0.005
DECAY_TARGET = 1e-2
FAST_DECAY_PCT = 0.3
SLOW_DECAY_PCT = 1.5
MIN_DECAY = math.log(DECAY_TARGET) / SLOW_DECAY_PCT
MAX_DECAY = math.log(DECAY_TARGET) / FAST_DECAY_PCT
D_FF = 11008
FFN_CONV = 3
NORM_EPS = 1e-6
IN_WIDTH = ATTN_WIDTH + 2 * KV_WIDTH + 3 * HYENA_WIDTH + 2 * D_MODEL
SPLIT_POINTS = (ATTN_WIDTH,
                ATTN_WIDTH + KV_WIDTH,
                ATTN_WIDTH + 2 * KV_WIDTH,
                ATTN_WIDTH + 2 * KV_WIDTH + 3 * HYENA_WIDTH,
                ATTN_WIDTH + 2 * KV_WIDTH + 3 * HYENA_WIDTH + D_MODEL)

kernel_name = "hybrid_gqa_hyena_gated_encoder_block"


def rmsnorm(x, g):
    xf = x.astype(jnp.float32)
    y = xf * lax.rsqrt(jnp.mean(xf * xf, axis=-1, keepdims=True) + NORM_EPS)
    return (y * g.astype(jnp.float32)).astype(x.dtype)


def depthwise_conv(x, w, b):
    width, ch = w.shape
    pad = (width - 1) // 2
    y = lax.conv_general_dilated(
        x, w[:, None, :].astype(x.dtype), window_strides=(1,), padding=[(pad, pad)],
        dimension_numbers=("NWC", "WIO", "NWC"), feature_group_count=ch)
    return y + b.astype(x.dtype)


def axial_rope_tables(rows):
    row = jnp.repeat(jnp.arange(rows, dtype=jnp.float32), GRID_W)
    col = jnp.tile(jnp.arange(GRID_W, dtype=jnp.float32), rows)
    meta = jnp.zeros((N_META,), jnp.float32)
    row = jnp.concatenate([meta, row])
    col = jnp.concatenate([meta, col])
    inv_freq = ROPE_THETA ** (-jnp.arange(ROPE_HALF, dtype=jnp.float32) * 2.0 / ROPE_AXIS_DIM)
    ang = jnp.stack([row[:, None] * inv_freq, col[:, None] * inv_freq], axis=1)
    return jnp.cos(ang), jnp.sin(ang)


def apply_rope(x, cos, sin):
    xf = x.astype(jnp.float32).reshape(*x.shape[:-1], 2, 2, ROPE_HALF)
    x1, x2 = xf[..., 0, :], xf[..., 1, :]
    c, s = cos[None, :, None], sin[None, :, None]
    out = jnp.stack([x1 * c - x2 * s, x2 * c + x1 * s], axis=-2)
    return out.reshape(x.shape).astype(x.dtype)


def gqa_attention(q, k, v):
    b, seq_len = q.shape[0], q.shape[1]
    n_real = seq_len - N_META
    n_blk = n_real // Q_BLOCK
    scale = HEAD_DIM ** -0.5
    q = q.reshape(b, seq_len, N_KV_HEADS, GQA_GROUP, HEAD_DIM)

    def attend(qb):
        s = jnp.einsum("btkgd,bskd->bkgts", qb, k).astype(jnp.float32) * scale
        p = jax.nn.softmax(s, axis=-1).astype(v.dtype)
        return jnp.einsum("bkgts,bskd->btkgd", p, v)

    o_meta = attend(q[:, :N_META])
    q_blocks = jnp.moveaxis(
        q[:, N_META:].reshape(b, n_blk, Q_BLOCK, N_KV_HEADS, GQA_GROUP, HEAD_DIM), 1, 0)
    o_real = jnp.moveaxis(lax.map(attend, q_blocks), 0, 1).reshape(
        b, n_real, N_KV_HEADS, GQA_GROUP, HEAD_DIM)
    return jnp.concatenate([o_meta, o_real], axis=1).reshape(b, seq_len, ATTN_WIDTH)


def implicit_filters(seq_len, w1, b1, w2, b2, w3, b3, w4, freq):
    dt = w1.dtype
    t = jnp.linspace(0.0, 1.0, seq_len, dtype=jnp.float32)[:, None]
    w = 2.0 * math.pi * jnp.arange(seq_len, dtype=jnp.float32)[:, None] / seq_len
    f = jnp.linspace(1e-4, FILTER_BANDS - 1, FILTER_BANDS, dtype=jnp.float32)[None, :]
    emb = jnp.concatenate([t, jnp.cos(f * w), -jnp.sin(f * w)], axis=-1).astype(dt)
    hid = jnp.sin(freq * (emb @ w1 + b1))
    hid = jnp.sin(freq * (hid @ w2 + b2))
    hid = jnp.sin(freq * (hid @ w3 + b3))
    filt = (hid @ w4).astype(jnp.float32).reshape(seq_len, HYENA_ORDER, 2, HYENA_WIDTH)
    deltas = jnp.linspace(MIN_DECAY, MAX_DECAY, HYENA_WIDTH, dtype=jnp.float32)
    decay = jnp.exp(-t * jnp.abs(deltas))
    return filt * decay[:, None, None, :]


def bidir_long_conv(z, h_fwd, h_bwd, d_skip):
    seq_len, ch = h_fwd.shape
    n_fft = 2 * seq_len
    k = jnp.concatenate([h_fwd, jnp.zeros((1, ch), h_fwd.dtype), h_bwd[:0:-1]], axis=0)
    k_f = jnp.fft.rfft(k, n=n_fft, axis=0)
    zf32 = z.astype(jnp.float32)
    z_f = jnp.fft.rfft(zf32, n=n_fft, axis=1)
    y = jnp.fft.irfft(z_f * k_f[None], n=n_fft, axis=1)[:, :seq_len]
    return (y + d_skip.astype(jnp.float32) * zf32).astype(z.dtype)


def setup_inputs(seed: int = 0) -> dict:
    key = jax.random.key(seed)
    ks = iter(jax.random.split(key, 32))

    def nrm(shape, scale):
        return jax.random.normal(next(ks), shape, jnp.float32) * scale

    def gain(shape):
        return 1.0 + 0.05 * jax.random.normal(next(ks), shape, jnp.float32)

    return {
        "x": nrm((BATCH, SEQ, D_MODEL), 1.0),
        "meta_tokens": nrm((N_META, D_MODEL), 1.0),
        "norm_mix": gain((DEPTH, D_MODEL)),
        "w_in": nrm((DEPTH, D_MODEL, IN_WIDTH), D_MODEL ** -0.5),
        "q_norm": gain((DEPTH, HEAD_DIM)),
        "k_norm": gain((DEPTH, HEAD_DIM)),
        "hyena_conv_w": nrm((DEPTH, HYENA_SHORT_CONV, 3 * HYENA_WIDTH), HYENA_SHORT_CONV ** -0.5),
        "hyena_conv_b": nrm((DEPTH, 3 * HYENA_WIDTH), 0.02),
        "filt_w1": nrm((DEPTH, FILTER_EMB_DIM, FILTER_HIDDEN), FILTER_EMB_DIM ** -0.5),
        "filt_b1": nrm((DEPTH, FILTER_HIDDEN), 0.1),
        "filt_w2": nrm((DEPTH, FILTER_HIDDEN, FILTER_HIDDEN), FILTER_HIDDEN ** -0.5),
        "filt_b2": nrm((DEPTH, FILTER_HIDDEN), 0.1),
        "filt_w3": nrm((DEPTH, FILTER_HIDDEN, FILTER_HIDDEN), FILTER_HIDDEN ** -0.5),
        "filt_b3": nrm((DEPTH, FILTER_HIDDEN), 0.1),
        "filt_w4": nrm((DEPTH, FILTER_HIDDEN, HYENA_ORDER * 2 * HYENA_WIDTH), FILTER_OUT_SCALE),
        "filt_freq": gain((DEPTH, FILTER_HIDDEN)),
        "hyena_skip": nrm((DEPTH, HYENA_ORDER, HYENA_WIDTH), 0.5),
        "w_attn_branch": nrm((DEPTH, ATTN_WIDTH, D_MODEL), ATTN_WIDTH ** -0.5),
        "w_hyena_branch": nrm((DEPTH, HYENA_WIDTH, D_MODEL), HYENA_WIDTH ** -0.5),
        "w_out": nrm((DEPTH, D_MODEL, D_MODEL), D_MODEL ** -0.5),
        "norm_ffn": gain((DEPTH, D_MODEL)),
        "w_ffn_gate": nrm((DEPTH, D_MODEL, D_FF), D_MODEL ** -0.5),
        "w_ffn_up": nrm((DEPTH, D_MODEL, D_FF), D_MODEL ** -0.5),
        "ffn_conv_w": nrm((DEPTH, FFN_CONV, D_FF), FFN_CONV ** -0.5),
        "ffn_conv_b": nrm((DEPTH, D_FF), 0.02),
        "w_ffn_down": nrm((DEPTH, D_FF, D_MODEL), D_FF ** -0.5),
        "norm_final": gain((D_MODEL,)),
    }


def reference(x, meta_tokens, norm_mix, w_in, q_norm, k_norm, hyena_conv_w, hyena_conv_b,
              filt_w1, filt_b1, filt_w2, filt_b2, filt_w3, filt_b3, filt_w4, filt_freq,
              hyena_skip, w_attn_branch, w_hyena_branch, w_out, norm_ffn, w_ffn_gate,
              w_ffn_up, ffn_conv_w, ffn_conv_b, w_ffn_down, norm_final):
    batch, n_tok, _ = x.shape
    rows = n_tok // GRID_W
    seq_len = n_tok + N_META
    meta = jnp.broadcast_to(meta_tokens[None].astype(x.dtype), (batch, N_META, D_MODEL))
    h = jnp.concatenate([meta, x], axis=1)
    cos, sin = axial_rope_tables(rows)

    for i in range(DEPTH):
        u = rmsnorm(h, norm_mix[i])
        proj = u @ w_in[i]
        q, k, v, hy, g_attn, g_hy = jnp.split(proj, SPLIT_POINTS, axis=-1)

        q = q.reshape(batch, seq_len, N_Q_HEADS, HEAD_DIM)
        k = k.reshape(batch, seq_len, N_KV_HEADS, HEAD_DIM)
        v = v.reshape(batch, seq_len, N_KV_HEADS, HEAD_DIM)
        q = apply_rope(rmsnorm(q, q_norm[i]), cos, sin)
        k = apply_rope(rmsnorm(k, k_norm[i]), cos, sin)
        y_attn = gqa_attention(q, k, v)

        hy = depthwise_conv(hy, hyena_conv_w[i], hyena_conv_b[i])
        z, x1, x2 = jnp.split(hy, 3, axis=-1)
        filters = implicit_filters(seq_len, filt_w1[i], filt_b1[i], filt_w2[i], filt_b2[i],
                                   filt_w3[i], filt_b3[i], filt_w4[i], filt_freq[i])
        for o, gate in enumerate((x1, x2)):
            z = gate * bidir_long_conv(z, filters[:, o, 0], filters[:, o, 1], hyena_skip[i, o])

        mixed = (jax.nn.sigmoid(g_attn) * (y_attn @ w_attn_branch[i])
                 + jax.nn.sigmoid(g_hy) * (z @ w_hyena_branch[i]))
        h = h + mixed @ w_out[i]

        u2 = rmsnorm(h, norm_ffn[i])
        gate = depthwise_conv(u2 @ w_ffn_gate[i], ffn_conv_w[i], ffn_conv_b[i])
        h = h + (jax.nn.silu(gate) * (u2 @ w_ffn_up[i])) @ w_ffn_down[i]

    return rmsnorm(h, norm_final)[:, N_META:]
```

```python
import functools
import math

import jax
import jax.numpy as jnp
from jax import lax
from jax.experimental import pallas as pl
from jax.experimental.pallas import tpu as pltpu

f32 = jnp.float32
bf16 = jnp.bfloat16

N_META_TOKENS = 16
GRID_W = 64
HEAD_DIM = 128
N_Q_HEADS = 16
N_KV_HEADS = 4
GQA_GROUP = N_Q_HEADS // N_KV_HEADS
ATTN_WIDTH = N_Q_HEADS * HEAD_DIM
KV_WIDTH = N_KV_HEADS * HEAD_DIM
ROPE_THETA = 10000.0
ROPE_HALF = HEAD_DIM // 4
FILTER_EMB_DIM = 33
FILTER_BANDS = (FILTER_EMB_DIM - 1) // 2
DECAY_TARGET = 1e-2
MIN_DECAY = math.log(DECAY_TARGET) / 1.5
MAX_DECAY = math.log(DECAY_TARGET) / 0.3
NORM_EPS = 1e-6

LANES = 128
SUBLANES = 8
V7X_VMEM_LIMIT_BYTES = 56 * 1024 * 1024
ROW_PAD = 512
MASK_BIAS = -1e30


def _round_up(x, m):
    return (x + m - 1) // m * m


def _pick(n, cands):
    for c in cands:
        if n % c == 0:
            return c
    return n


def _params(*sem):
    return pltpu.CompilerParams(dimension_semantics=sem, vmem_limit_bytes=V7X_VMEM_LIMIT_BYTES)


def _split(x):
    hi = x.astype(bf16)
    lo = (x - hi.astype(f32)).astype(bf16)
    return hi, lo


def _dot(a, b):
    return jnp.dot(a, b, preferred_element_type=f32)


def _dot3(ah, al, bh, bl):
    return _dot(ah, bh) + _dot(ah, bl) + _dot(al, bh)


def _rmsnorm_kernel(x_ref, g_ref, o_ref, *, n_valid):
    tr = x_ref.shape[0]
    x = x_ref[...]
    y = x * lax.rsqrt(jnp.mean(x * x, axis=-1, keepdims=True) + NORM_EPS) * g_ref[...]
    rows = pl.program_id(0) * tr + lax.broadcasted_iota(jnp.int32, (tr, 1), 0)
    o_ref[...] = jnp.where(rows < n_valid, y, 0.0).astype(o_ref.dtype)


def _rmsnorm(x, g, n_valid, out_dtype):
    m, d = x.shape
    tr = _pick(m, (256, 128, 64, 32, 16))
    return pl.pallas_call(
        functools.partial(_rmsnorm_kernel, n_valid=n_valid),
        grid=(m // tr,),
        in_specs=[pl.BlockSpec((tr, d), lambda i: (i, 0)),
                  pl.BlockSpec((1, d), lambda i: (0, 0))],
        out_specs=pl.BlockSpec((tr, d), lambda i: (i, 0)),
        out_shape=jax.ShapeDtypeStruct((m, d), out_dtype),
        compiler_params=_params("parallel"),
        name="rmsnorm",
    )(x, g.reshape(1, d))


def _mm_kernel(a_ref, b_ref, o_ref):
    o_ref[...] = _dot(a_ref[...], b_ref[...]).astype(o_ref.dtype)


def _mm_res_kernel(a_ref, b_ref, r_ref, o_ref):
    o_ref[...] = r_ref[...] + _dot(a_ref[...], b_ref[...])


def _mm_pair_kernel(a_ref, bg_ref, bu_ref, g_ref, u_ref):
    a = a_ref[...]
    g_ref[...] = _dot(a, bg_ref[...])
    u_ref[...] = _dot(a, bu_ref[...]).astype(u_ref.dtype)


def _merge_kernel(ya_ref, zb_ref, wa_ref, wb_ref, ga_ref, gb_ref, o_ref):
    a = _dot(ya_ref[...], wa_ref[...])
    b = _dot(zb_ref[...], wb_ref[...])
    o_ref[...] = (jax.nn.sigmoid(ga_ref[...]) * a + jax.nn.sigmoid(gb_ref[...]) * b).astype(o_ref.dtype)


def _matmul(a, b, out_dtype, *, tm, tn, res=None, name):
    m, k = a.shape
    n = b.shape[1]
    in_specs = [pl.BlockSpec((tm, k), lambda j, i: (i, 0)),
                pl.BlockSpec((k, tn), lambda j, i: (0, j))]
    args = [a, b]
    kern = _mm_kernel
    if res is not None:
        in_specs.append(pl.BlockSpec((tm, tn), lambda j, i: (i, j)))
        args.append(res)
        kern = _mm_res_kernel
    return pl.pallas_call(
        kern,
        grid=(n // tn, m // tm),
        in_specs=in_specs,
        out_specs=pl.BlockSpec((tm, tn), lambda j, i: (i, j)),
        out_shape=jax.ShapeDtypeStruct((m, n), out_dtype),
        compiler_params=_params("parallel", "parallel"),
        name=name,
    )(*args)


def _matmul_pair(a, bg, bu, *, tm, tn):
    m, k = a.shape
    n = bg.shape[1]
    return pl.pallas_call(
        _mm_pair_kernel,
        grid=(m // tm, n // tn),
        in_specs=[pl.BlockSpec((tm, k), lambda i, j: (i, 0)),
                  pl.BlockSpec((k, tn), lambda i, j: (0, j)),
                  pl.BlockSpec((k, tn), lambda i, j: (0, j))],
        out_specs=[pl.BlockSpec((tm, tn), lambda i, j: (i, j)),
                   pl.BlockSpec((tm, tn), lambda i, j: (i, j))],
        out_shape=[jax.ShapeDtypeStruct((m, n), f32), jax.ShapeDtypeStruct((m, n), bf16)],
        compiler_params=_params("parallel", "parallel"),
        name="ffn_gate_up",
    )(a, bg, bu)


def _merge(ya, zb, wa, wb, proj, ga_off, gb_off, *, tm, tn):
    m, ka = ya.shape
    kb = zb.shape[1]
    n = wa.shape[1]
    ga_blk, gb_blk = ga_off // tn, gb_off // tn
    return pl.pallas_call(
        _merge_kernel,
        grid=(n // tn, m // tm),
        in_specs=[pl.BlockSpec((tm, ka), lambda j, i: (i, 0)),
                  pl.BlockSpec((tm, kb), lambda j, i: (i, 0)),
                  pl.BlockSpec((ka, tn), lambda j, i: (0, j)),
                  pl.BlockSpec((kb, tn), lambda j, i: (0, j)),
                  pl.BlockSpec((tm, tn), lambda j, i: (i, ga_blk + j)),
                  pl.BlockSpec((tm, tn), lambda j, i: (i, gb_blk + j))],
        out_specs=pl.BlockSpec((tm, tn), lambda j, i: (i, j)),
        out_shape=jax.ShapeDtypeStruct((m, n), bf16),
        compiler_params=_params("parallel", "parallel"),
        name="gated_merge",
    )(ya, zb, wa, wb, proj, proj)


def _rope_tables(n_tok, lp):
    rows = n_tok // GRID_W
    row = jnp.repeat(jnp.arange(rows, dtype=f32), GRID_W)
    col = jnp.tile(jnp.arange(GRID_W, dtype=f32), rows)
    lead = jnp.zeros((N_META_TOKENS,), f32)
    tail = jnp.zeros((lp - N_META_TOKENS - n_tok,), f32)
    row = jnp.concatenate([lead, row, tail])
    col = jnp.concatenate([lead, col, tail])
    inv_freq = ROPE_THETA ** (-jnp.arange(ROPE_HALF, dtype=f32) * 2.0 / (2 * ROPE_HALF))
    ar, ac = row[:, None] * inv_freq, col[:, None] * inv_freq
    cr, sr, cc, sc = jnp.cos(ar), jnp.sin(ar), jnp.cos(ac), jnp.sin(ac)
    zero = jnp.zeros_like(sr)
    cos = jnp.concatenate([cr, cr, cc, cc], axis=-1)
    sin_up = jnp.concatenate([-sr, zero, -sc, zero], axis=-1)
    sin_dn = jnp.concatenate([zero, sr, zero, sc], axis=-1)
    return cos, sin_up, sin_dn


def _qk_prep_kernel(x_ref, g_ref, cos_ref, sup_ref, sdn_ref, o_ref):
    x = x_ref[...]
    y = x * lax.rsqrt(jnp.mean(x * x, axis=-1, keepdims=True) + NORM_EPS) * g_ref[...]
    out = (y * cos_ref[...]
           + pltpu.roll(y, HEAD_DIM - ROPE_HALF, 1) * sup_ref[...]
           + pltpu.roll(y, ROPE_HALF, 1) * sdn_ref[...])
    o_ref[...] = out.astype(o_ref.dtype)


def _qk_prep(proj, gains, cos, sin_up, sin_dn):
    lp = proj.shape[0]
    n_heads = N_Q_HEADS + N_KV_HEADS
    tr = _pick(lp, (512, 256, 128))
    tab = pl.BlockSpec((tr, HEAD_DIM), lambda i, h: (i, 0))
    return pl.pallas_call(
        _qk_prep_kernel,
        grid=(lp // tr, n_heads),
        in_specs=[pl.BlockSpec((tr, HEAD_DIM), lambda i, h: (i, h)),
                  pl.BlockSpec((None, 1, HEAD_DIM), lambda i, h: (h, 0, 0)),
                  tab, tab, tab],
        out_specs=pl.BlockSpec((tr, HEAD_DIM), lambda i, h: (i, h)),
        out_shape=jax.ShapeDtypeStruct((lp, n_heads * HEAD_DIM), bf16),
        compiler_params=_params("parallel", "parallel"),
        name="qk_norm_rope",
    )(proj, gains, cos, sin_up, sin_dn)


def _flash_kernel(q_ref, k_ref, v_ref, bias_ref, o_ref, m_scr, l_scr, acc_scr, *, scale):
    j = pl.program_id(2)

    @pl.when(j == 0)
    def _():
        m_scr[...] = jnp.full(m_scr.shape, -jnp.inf, f32)
        l_scr[...] = jnp.zeros(l_scr.shape, f32)
        acc_scr[...] = jnp.zeros(acc_scr.shape, f32)

    k = k_ref[...]
    v = v_ref[...]
    bias = bias_ref[...]
    for g in range(GQA_GROUP):
        q = q_ref[:, g * HEAD_DIM:(g + 1) * HEAD_DIM]
        s = lax.dot_general(q, k, (((1,), (1,)), ((), ())), preferred_element_type=f32) * scale + bias
        m_prev = m_scr[g]
        m_new = jnp.maximum(m_prev, jnp.max(s, axis=-1, keepdims=True))
        alpha = jnp.exp(m_prev - m_new)
        p = jnp.exp(s - m_new)
        l_scr[g] = alpha * l_scr[g] + jnp.sum(p, axis=-1, keepdims=True)
        acc_scr[g] = alpha * acc_scr[g] + _dot(p.astype(bf16), v)
        m_scr[g] = m_new

    @pl.when(j == pl.num_programs(2) - 1)
    def _():
        for g in range(GQA_GROUP):
            o_ref[:, g * HEAD_DIM:(g + 1) * HEAD_DIM] = (acc_scr[g] / l_scr[g]).astype(o_ref.dtype)


def _flash_attention(qk, v, key_bias, *, tq, tk):
    lp = qk.shape[0]
    gw = GQA_GROUP * HEAD_DIM
    return pl.pallas_call(
        functools.partial(_flash_kernel, scale=HEAD_DIM ** -0.5),
        grid=(N_KV_HEADS, lp // tq, lp // tk),
        in_specs=[pl.BlockSpec((tq, gw), lambda h, i, j: (i, h)),
                  pl.BlockSpec((tk, HEAD_DIM), lambda h, i, j: (j, N_Q_HEADS + h)),
                  pl.BlockSpec((tk, HEAD_DIM), lambda h, i, j: (j, h)),
                  pl.BlockSpec((1, tk), lambda h, i, j: (0, j))],
        out_specs=pl.BlockSpec((tq, gw), lambda h, i, j: (i, h)),
        out_shape=jax.ShapeDtypeStruct((lp, ATTN_WIDTH), bf16),
        scratch_shapes=[pltpu.VMEM((GQA_GROUP, tq, 1), f32),
                        pltpu.VMEM((GQA_GROUP, tq, 1), f32),
                        pltpu.VMEM((GQA_GROUP, tq, HEAD_DIM), f32)],
        compiler_params=_params("parallel", "parallel", "arbitrary"),
        name="flash_gqa",
    )(qk, qk, v, key_bias)


def _conv3_rows(x, prev, nxt, w, row0, n_valid):
    tr = x.shape[0]
    loc = lax.broadcasted_iota(jnp.int32, (tr, 1), 0)
    t = loc + row0
    xm = jnp.where(loc == 0, prev[SUBLANES - 1:SUBLANES, :], pltpu.roll(x, 1, 0))
    xm = jnp.where(t == 0, 0.0, xm)
    xp = jnp.where(loc == tr - 1, nxt[0:1, :], pltpu.roll(x, tr - 1, 0))
    xp = jnp.where(t >= n_valid - 1, 0.0, xp)
    return w[0:1, :] * xm + w[1:2, :] * x + w[2:3, :] * xp


def _halo_specs(tr, tc, n_rows, col_blk0):
    rb = tr // SUBLANES
    last = n_rows // SUBLANES - 1
    return [pl.BlockSpec((tr, tc), lambda i, j: (jnp.minimum(i, n_rows // tr - 1), col_blk0 + j)),
            pl.BlockSpec((SUBLANES, tc), lambda i, j: (jnp.clip(i * rb - 1, 0, last), col_blk0 + j)),
            pl.BlockSpec((SUBLANES, tc), lambda i, j: (jnp.minimum((i + 1) * rb, last), col_blk0 + j))]


def _hyena_conv_kernel(x_ref, p_ref, n_ref, w_ref, b_ref, o_ref, *, n_valid):
    tr = x_ref.shape[0]
    row0 = pl.program_id(0) * tr
    y = _conv3_rows(x_ref[...], p_ref[...], n_ref[...], w_ref[...], row0, n_valid) + b_ref[...]
    t = row0 + lax.broadcasted_iota(jnp.int32, (tr, 1), 0)
    o_ref[...] = jnp.where(t < n_valid, y, 0.0)


def _hyena_conv(proj, w, b, col_off, n_valid, out_rows):
    lp = proj.shape[0]
    width = w.shape[1]
    tc = _pick(math.gcd(col_off, width), (1024, 512, 256, 128))
    tr = _pick(math.gcd(lp, out_rows), (512, 256, 128))
    return pl.pallas_call(
        functools.partial(_hyena_conv_kernel, n_valid=n_valid),
        grid=(out_rows // tr, width // tc),
        in_specs=_halo_specs(tr, tc, lp, col_off // tc) + [
            pl.BlockSpec((3, tc), lambda i, j: (0, j)),
            pl.BlockSpec((1, tc), lambda i, j: (0, j))],
        out_specs=pl.BlockSpec((tr, tc), lambda i, j: (i, j)),
        out_shape=jax.ShapeDtypeStruct((out_rows, width), f32),
        compiler_params=_params("parallel", "parallel"),
        name="hyena_short_conv",
    )(proj, proj, proj, w, b.reshape(1, width))


def _ffn_act_kernel(g_ref, p_ref, n_ref, u_ref, w_ref, b_ref, o_ref, *, n_valid):
    tr = g_ref.shape[0]
    row0 = pl.program_id(0) * tr
    gate = _conv3_rows(g_ref[...], p_ref[...], n_ref[...], w_ref[...], row0, n_valid) + b_ref[...]
    o_ref[...] = (gate * jax.nn.sigmoid(gate) * u_ref[...].astype(f32)).astype(o_ref.dtype)


def _ffn_act(gate_pre, up, w, b, n_valid):
    lp, width = gate_pre.shape
    tc = _pick(width, (256, 128))
    tr = _pick(lp, (1536, 1024, 512, 256, 128))
    return pl.pallas_call(
        functools.partial(_ffn_act_kernel, n_valid=n_valid),
        grid=(lp // tr, width // tc),
        in_specs=_halo_specs(tr, tc, lp, 0) + [
            pl.BlockSpec((tr, tc), lambda i, j: (i, j)),
            pl.BlockSpec((3, tc), lambda i, j: (0, j)),
            pl.BlockSpec((1, tc), lambda i, j: (0, j))],
        out_specs=pl.BlockSpec((tr, tc), lambda i, j: (i, j)),
        out_shape=jax.ShapeDtypeStruct((lp, width), bf16),
        compiler_params=_params("parallel", "parallel"),
        name="ffn_conv_silu_gate",
    )(gate_pre, gate_pre, gate_pre, up, w, b.reshape(1, width))


EMB_COLS = LANES
COL_FWD = FILTER_EMB_DIM
COL_BWD = FILTER_EMB_DIM + 1
COL_T = FILTER_EMB_DIM + 2


def _filter_inputs(seq_len, n_fft, n_rows):
    t = jnp.linspace(0.0, 1.0, seq_len, dtype=f32)[:, None]
    w = 2.0 * math.pi * jnp.arange(seq_len, dtype=f32)[:, None] / seq_len
    f = jnp.linspace(1e-4, FILTER_BANDS - 1, FILTER_BANDS, dtype=f32)[None, :]
    emb = jnp.concatenate([t, jnp.cos(f * w), -jnp.sin(f * w)], axis=-1)
    n = jnp.arange(n_rows)
    fwd = n < seq_len
    bwd = (n > n_fft - seq_len) & (n < n_fft)
    lag = jnp.where(fwd, n, jnp.where(bwd, n_fft - n, 0))
    cols = [emb[lag], fwd.astype(f32)[:, None], bwd.astype(f32)[:, None], t[lag],
            jnp.zeros((n_rows, EMB_COLS - FILTER_EMB_DIM - 3), f32)]
    return jnp.concatenate(cols, axis=-1)


def _filter_kernel(e_ref, w1_ref, b1_ref, w2_ref, b2_ref, w3_ref, b3_ref, w4_ref, fr_ref, ad_ref, o_ref):
    e = e_ref[...]
    freq = fr_ref[...]

    def layer(x, w_ref, b_ref):
        xh, xl = _split(x)
        wh, wl = _split(w_ref[...])
        return jnp.sin(freq * (_dot3(xh, xl, wh, wl) + b_ref[...]))

    hid = layer(layer(layer(e, w1_ref, b1_ref), w2_ref, b2_ref), w3_ref, b3_ref)
    hh, hl = _split(hid)
    fwd = e[:, COL_FWD:COL_FWD + 1]
    bwd = e[:, COL_BWD:COL_BWD + 1]
    decay = jnp.exp(-e[:, COL_T:COL_T + 1] * ad_ref[...])
    for o in range(2):
        wfh, wfl = _split(w4_ref[o, 0])
        wbh, wbl = _split(w4_ref[o, 1])
        o_ref[o] = (fwd * _dot3(hh, hl, wfh, wfl) + bwd * _dot3(hh, hl, wbh, wbl)) * decay


def _filters(e_ext, w1, b1, w2, b2, w3, b3, w4, freq, n_ch):
    n_rows = e_ext.shape[0]
    hid = w2.shape[0]
    tr = LANES
    w1p = jnp.zeros((EMB_COLS, hid), f32).at[:FILTER_EMB_DIM].set(w1)
    w4r = w4.reshape(hid, 2, 2, n_ch).transpose(1, 2, 0, 3)
    abs_delta = jnp.abs(jnp.linspace(MIN_DECAY, MAX_DECAY, n_ch, dtype=f32)).reshape(1, n_ch)
    full = lambda *shape: pl.BlockSpec(shape, lambda i: (0,) * len(shape))
    return pl.pallas_call(
        _filter_kernel,
        grid=(n_rows // tr,),
        in_specs=[pl.BlockSpec((tr, EMB_COLS), lambda i: (i, 0)),
                  full(EMB_COLS, hid), full(1, hid), full(hid, hid), full(1, hid),
                  full(hid, hid), full(1, hid), full(2, 2, hid, n_ch), full(1, hid), full(1, n_ch)],
        out_specs=pl.BlockSpec((2, tr, n_ch), lambda i: (0, i, 0)),
        out_shape=jax.ShapeDtypeStruct((2, n_rows, n_ch), f32),
        compiler_params=_params("parallel"),
        name="hyena_filters",
    )(e_ext, w1p, b1.reshape(1, hid), w2, b2.reshape(1, hid), w3, b3.reshape(1, hid), w4r,
      freq.reshape(1, hid), abs_delta)


def _dft_tables(n1, nz):
    n_fft = n1 * LANES
    k1n = (n1 + 1) // 2
    two_pi = 2.0 * math.pi

    def angle(m, period):
        m = jnp.where(m > period // 2, m - period, m)
        return m.astype(f32) * (two_pi / period)

    k1 = jnp.arange(k1n, dtype=jnp.int32)
    nn = jnp.arange(n1, dtype=jnp.int32)
    a1 = angle((k1[:, None] * nn[None, :]) % n1, n1)
    fwd = jnp.stack([jnp.cos(a1), -jnp.sin(a1)], axis=1).reshape(2 * k1n, n1)
    wgt = jnp.where(k1 == 0, 1.0, 2.0).astype(f32) / n_fft
    inv = (jnp.stack([jnp.cos(a1), -jnp.sin(a1)], axis=1) * wgt[:, None, None])
    inv = inv.reshape(2 * k1n, n1).T[:nz]

    n2 = jnp.arange(LANES, dtype=jnp.int32)
    k2 = jnp.arange(LANES, dtype=jnp.int32)
    m = (n2[None, None, :] * (k2[None, :, None] * n1 + k1[:, None, None])) % n_fft
    a2 = angle(m, n_fft)
    gr, gi = jnp.cos(a2), -jnp.sin(a2)
    g = jnp.concatenate([jnp.concatenate([gr, -gi], axis=2),
                         jnp.concatenate([gi, gr], axis=2)], axis=1)
    grt, git = jnp.swapaxes(gr, 1, 2), jnp.swapaxes(gi, 1, 2)
    ginv = jnp.concatenate([jnp.concatenate([grt, git], axis=2),
                            jnp.concatenate([-git, grt], axis=2)], axis=1)
    return _split(fwd), _split(g), _split(ginv), _split(inv)


def _strided_col_map(n_slots, slot, ct):
    def index_map(jj):
        return (0, (jj // ct) * (n_slots * ct) + slot * ct + jj % ct)
    return index_map


def _dft_rows_kernel(mh_ref, ml_ref, x_ref, o_ref):
    xh, xl = _split(x_ref[...])
    o_ref[...] = _dot3(mh_ref[...], ml_ref[...], xh, xl)


def _dft_rows(mat, x2d, n_ch, *, n_slots, slot, tw):
    mh, ml = mat
    r, kd = mh.shape
    width = LANES * n_ch
    return pl.pallas_call(
        _dft_rows_kernel,
        grid=(width // tw,),
        in_specs=[pl.BlockSpec((r, kd), lambda jj: (0, 0)),
                  pl.BlockSpec((r, kd), lambda jj: (0, 0)),
                  pl.BlockSpec((kd, tw), _strided_col_map(n_slots, slot, n_ch // tw))],
        out_specs=pl.BlockSpec((r, tw), lambda jj: (0, jj)),
        out_shape=jax.ShapeDtypeStruct((r, width), f32),
        compiler_params=_params("parallel"),
        name="dft_outer_fwd",
    )(mh, ml, x2d)


def _spectrum_kernel(gh_ref, gl_ref, a_ref, o_ref):
    ah, al = _split(a_ref[...])
    o_ref[...] = _dot3(gh_ref[...], gl_ref[...], ah, al)


def _spectrum(g, a3, *, tc):
    gh, gl = g
    k1n, two_l, n_ch = a3.shape
    gspec = pl.BlockSpec((None, two_l, two_l), lambda k, c: (k, 0, 0))
    dspec = pl.BlockSpec((None, two_l, tc), lambda k, c: (k, 0, c))
    return pl.pallas_call(
        _spectrum_kernel,
        grid=(k1n, n_ch // tc),
        in_specs=[gspec, gspec, dspec],
        out_specs=dspec,
        out_shape=jax.ShapeDtypeStruct(a3.shape, f32),
        compiler_params=_params("parallel", "parallel"),
        name="dft_inner_fwd",
    )(gh, gl, a3)


def _spectral_conv_kernel(gh_ref, gl_ref, ih_ref, il_ref, a_ref, kf_ref, o_ref):
    ah, al = _split(a_ref[...])
    x = _dot3(gh_ref[...], gl_ref[...], ah, al)
    kf = kf_ref[...]
    xr, xi = x[:LANES], x[LANES:]
    kr, ki = kf[:LANES], kf[LANES:]
    y = jnp.concatenate([xr * kr - xi * ki, xr * ki + xi * kr], axis=0)
    yh, yl = _split(y)
    o_ref[...] = _dot3(ih_ref[...], il_ref[...], yh, yl)


def _spectral_conv(g, ginv, a3, kf3, *, tc):
    gh, gl = g
    ih, il = ginv
    k1n, two_l, n_ch = a3.shape
    gspec = pl.BlockSpec((None, two_l, two_l), lambda k, c: (k, 0, 0))
    dspec = pl.BlockSpec((None, two_l, tc), lambda k, c: (k, 0, c))
    return pl.pallas_call(
        _spectral_conv_kernel,
        grid=(k1n, n_ch // tc),
        in_specs=[gspec, gspec, gspec, gspec, dspec, dspec],
        out_specs=dspec,
        out_shape=jax.ShapeDtypeStruct(a3.shape, f32),
        compiler_params=_params("parallel", "parallel"),
        name="dft_inner_conv",
    )(gh, gl, ih, il, a3, kf3)


def _idft_gate_kernel(mh_ref, ml_ref, b_ref, z_ref, gate_ref, skip_ref, o_ref):
    bh, bl = _split(b_ref[...])
    y = _dot3(mh_ref[...], ml_ref[...], bh, bl)
    o_ref[...] = (gate_ref[...] * (y + skip_ref[...] * z_ref[...])).astype(o_ref.dtype)


def _idft_gate(mat, b2d, z2d, z_slots, hyc2d, gate_slot, skip, out_dtype, *, tw):
    mh, ml = mat
    nz, r = mh.shape
    n_ch = skip.shape[1]
    width = LANES * n_ch
    ct = n_ch // tw
    return pl.pallas_call(
        _idft_gate_kernel,
        grid=(width // tw,),
        in_specs=[pl.BlockSpec((nz, r), lambda jj: (0, 0)),
                  pl.BlockSpec((nz, r), lambda jj: (0, 0)),
                  pl.BlockSpec((r, tw), lambda jj: (0, jj)),
                  pl.BlockSpec((nz, tw), _strided_col_map(z_slots, 0, ct)),
                  pl.BlockSpec((nz, tw), _strided_col_map(3, gate_slot, ct)),
                  pl.BlockSpec((1, tw), lambda jj: (0, jj % ct))],
        out_specs=pl.BlockSpec((nz, tw), lambda jj: (0, jj)),
        out_shape=jax.ShapeDtypeStruct((nz, width), out_dtype),
        compiler_params=_params("parallel"),
        name="dft_outer_inv_gate",
    )(mh, ml, b2d, z2d, hyc2d, skip)


def kernel(x, meta_tokens, norm_mix, w_in, q_norm, k_norm, hyena_conv_w, hyena_conv_b, filt_w1, filt_b1, filt_w2, filt_b2, filt_w3, filt_b3, filt_w4, filt_freq, hyena_skip, w_attn_branch, w_hyena_branch, w_out, norm_ffn, w_ffn_gate, w_ffn_up, ffn_conv_w, ffn_conv_b, w_ffn_down, norm_final):
    batch, n_tok, d_model = x.shape
    depth = norm_mix.shape[0]
    n_ch = w_hyena_branch.shape[1]
    d_ff = w_ffn_gate.shape[2]
    seq_len = n_tok + N_META_TOKENS
    lp = _round_up(seq_len, ROW_PAD)
    nz = lp // LANES
    n1 = -(-(2 * seq_len - 1) // LANES) | 1
    n_fft = n1 * LANES
    k1n = (n1 + 1) // 2
    hy_off = ATTN_WIDTH + 2 * KV_WIDTH
    ga_off = hy_off + 3 * n_ch
    gb_off = ga_off + d_model

    cos, sin_up, sin_dn = _rope_tables(n_tok, lp)
    key_bias = jnp.where(jnp.arange(lp) < seq_len, 0.0, MASK_BIAS).astype(f32).reshape(1, lp)
    e_ext = _filter_inputs(seq_len, n_fft, n_fft)
    fwd_m, g_m, ginv_m, inv_m = _dft_tables(n1, nz)
    fwd_z = (fwd_m[0][:, :nz], fwd_m[1][:, :nz])

    tm = _pick(lp, (512, 256, 128))
    tq = _pick(lp, (512, 256, 128))
    tk = _pick(lp, (1536, 1024, 512, 256, 128))
    assert lp - tk < seq_len, "every key tile must hold at least one real token"
    tw = _pick(n_ch, (2048, 1024, 512, 256, 128))

    outs = []
    for b in range(batch):
        h = jnp.concatenate([meta_tokens.astype(f32), x[b],
                             jnp.zeros((lp - seq_len, d_model), f32)], axis=0)
        for i in range(depth):
            u = _rmsnorm(h, norm_mix[i], seq_len, bf16)
            in_w = w_in[i].shape[1]
            proj = _matmul(u, w_in[i].astype(bf16), f32, tm=tm,
                           tn=_pick(in_w, (1024, 512, 256, 128)), name="in_proj")

            gains = jnp.concatenate([jnp.broadcast_to(q_norm[i], (N_Q_HEADS, HEAD_DIM)),
                                     jnp.broadcast_to(k_norm[i], (N_KV_HEADS, HEAD_DIM))])
            qk = _qk_prep(proj, gains.reshape(-1, 1, HEAD_DIM), cos, sin_up, sin_dn)
            v = proj[:, ATTN_WIDTH + KV_WIDTH:hy_off].astype(bf16)
            y_attn = _flash_attention(qk, v, key_bias, tq=tq, tk=tk)

            hyc = _hyena_conv(proj, hyena_conv_w[i], hyena_conv_b[i], hy_off, seq_len, lp)
            hyc2d = hyc.reshape(nz, LANES * 3 * n_ch)
            filt = _filters(e_ext, filt_w1[i], filt_b1[i], filt_w2[i], filt_b2[i], filt_w3[i],
                            filt_b3[i], filt_w4[i], filt_freq[i], n_ch)
            z2d, z_slots = hyc2d, 3
            for o in range(2):
                kf_a = _dft_rows(fwd_m, filt[o].reshape(n1, LANES * n_ch), n_ch,
                                 n_slots=1, slot=0, tw=tw)
                kf3 = _spectrum(g_m, kf_a.reshape(k1n, 2 * LANES, n_ch), tc=tw)
                za = _dft_rows(fwd_z, z2d, n_ch, n_slots=z_slots, slot=0, tw=tw)
                zb = _spectral_conv(g_m, ginv_m, za.reshape(k1n, 2 * LANES, n_ch), kf3, tc=tw)
                z2d = _idft_gate(inv_m, zb.reshape(2 * k1n, LANES * n_ch), z2d, z_slots, hyc2d,
                                 1 + o, hyena_skip[i, o].reshape(1, n_ch),
                                 f32 if o == 0 else bf16, tw=tw)
                z_slots = 1
            z = z2d.reshape(lp, n_ch)

            tn_m = _pick(math.gcd(math.gcd(ga_off, gb_off), d_model), (1024, 512, 256, 128))
            mixed = _merge(y_attn, z, w_attn_branch[i].astype(bf16), w_hyena_branch[i].astype(bf16),
                           proj, ga_off, gb_off, tm=tm, tn=tn_m)
            h = _matmul(mixed, w_out[i].astype(bf16), f32, tm=tm,
                        tn=_pick(d_model, (1024, 512, 256, 128)), res=h, name="out_proj")

            u2 = _rmsnorm(h, norm_ffn[i], seq_len, bf16)
            gate_pre, up = _matmul_pair(u2, w_ffn_gate[i].astype(bf16), w_ffn_up[i].astype(bf16),
                                        tm=_pick(lp, (1536, 1024, 512, 256, 128)),
                                        tn=_pick(d_ff, (256, 128)))
            act = _ffn_act(gate_pre, up, ffn_conv_w[i], ffn_conv_b[i], seq_len)
            h = _matmul(act, w_ffn_down[i].astype(bf16), f32, tm=tm,
                        tn=_pick(d_model, (512, 256, 128)), res=h, name="ffn_down")
        out = _rmsnorm(h, norm_final, lp, f32)
        outs.append(out[N_META_TOKENS:seq_len])
    return jnp.stack(outs)
```

```python
import functools
import math

import jax
import jax.numpy as jnp
from jax import lax
from jax.experimental import pallas as pl
from jax.experimental.pallas import tpu as pltpu

f32 = jnp.float32
bf16 = jnp.bfloat16

N_META_TOKENS = 16
GRID_W = 64
HEAD_DIM = 128
N_Q_HEADS = 16
N_KV_HEADS = 4
GQA_GROUP = N_Q_HEADS // N_KV_HEADS
ATTN_WIDTH = N_Q_HEADS * HEAD_DIM
KV_WIDTH = N_KV_HEADS * HEAD_DIM
ROPE_THETA = 10000.0
ROPE_HALF = HEAD_DIM // 4
FILTER_EMB_DIM = 33
FILTER_BANDS = (FILTER_EMB_DIM - 1) // 2
DECAY_TARGET = 1e-2
MIN_DECAY = math.log(DECAY_TARGET) / 1.5
MAX_DECAY = math.log(DECAY_TARGET) / 0.3
NORM_EPS = 1e-6

LANES = 128
SUBLANES = 8
V7X_VMEM_LIMIT_BYTES = 56 * 1024 * 1024
ROW_PAD = 512
MASK_BIAS = -1e30


def _round_up(x, m):
    return (x + m - 1) // m * m


def _pick(n, cands):
    for c in cands:
        if n % c == 0:
            return c
    return n


def _params(*sem):
    return pltpu.CompilerParams(dimension_semantics=sem, vmem_limit_bytes=V7X_VMEM_LIMIT_BYTES)


def _split(x):
    hi = x.astype(bf16)
    lo = (x - hi.astype(f32)).astype(bf16)
    return hi, lo


def _dot(a, b):
    return jnp.dot(a, b, preferred_element_type=f32)


def _dot3(ah, al, bh, bl):
    return _dot(ah, bh) + _dot(ah, bl) + _dot(al, bh)


def _rmsnorm_kernel(x_ref, g_ref, o_ref, *, n_valid):
    tr = x_ref.shape[0]
    x = x_ref[...]
    y = x * lax.rsqrt(jnp.mean(x * x, axis=-1, keepdims=True) + NORM_EPS) * g_ref[...]
    rows = pl.program_id(0) * tr + lax.broadcasted_iota(jnp.int32, (tr, 1), 0)
    o_ref[...] = jnp.where(rows < n_valid, y, 0.0).astype(o_ref.dtype)


def _rmsnorm(x, g, n_valid, out_dtype):
    m, d = x.shape
    tr = _pick(m, (256, 128, 64, 32, 16))
    return pl.pallas_call(
        functools.partial(_rmsnorm_kernel, n_valid=n_valid),
        grid=(m // tr,),
        in_specs=[pl.BlockSpec((tr, d), lambda i: (i, 0)),
                  pl.BlockSpec((1, d), lambda i: (0, 0))],
        out_specs=pl.BlockSpec((tr, d), lambda i: (i, 0)),
        out_shape=jax.ShapeDtypeStruct((m, d), out_dtype),
        compiler_params=_params("parallel"),
        name="rmsnorm",
    )(x, g.reshape(1, d))


def _mm_kernel(a_ref, b_ref, o_ref):
    o_ref[...] = _dot(a_ref[...], b_ref[...]).astype(o_ref.dtype)


def _mm_res_kernel(a_ref, b_ref, r_ref, o_ref):
    o_ref[...] = r_ref[...] + _dot(a_ref[...], b_ref[...])


def _mm_pair_kernel(a_ref, bg_ref, bu_ref, g_ref, u_ref):
    a = a_ref[...]
    g_ref[...] = _dot(a, bg_ref[...])
    u_ref[...] = _dot(a, bu_ref[...]).astype(u_ref.dtype)


def _merge_kernel(ya_ref, zb_ref, wa_ref, wb_ref, ga_ref, gb_ref, o_ref):
    a = _dot(ya_ref[...], wa_ref[...])
    b = _dot(zb_ref[...].astype(bf16), wb_ref[...])
    o_ref[...] = (jax.nn.sigmoid(ga_ref[...]) * a + jax.nn.sigmoid(gb_ref[...]) * b).astype(o_ref.dtype)


def _matmul(a, b, out_dtype, *, tm, tn, res=None, name):
    m, k = a.shape
    n = b.shape[1]
    in_specs = [pl.BlockSpec((tm, k), lambda j, i: (i, 0)),
                pl.BlockSpec((k, tn), lambda j, i: (0, j))]
    args = [a, b]
    kern = _mm_kernel
    if res is not None:
        in_specs.append(pl.BlockSpec((tm, tn), lambda j, i: (i, j)))
        args.append(res)
        kern = _mm_res_kernel
    return pl.pallas_call(
        kern,
        grid=(n // tn, m // tm),
        in_specs=in_specs,
        out_specs=pl.BlockSpec((tm, tn), lambda j, i: (i, j)),
        out_shape=jax.ShapeDtypeStruct((m, n), out_dtype),
        compiler_params=_params("parallel", "parallel"),
        name=name,
    )(*args)


def _matmul_pair(a, bg, bu, *, tm, tn):
    m, k = a.shape
    n = bg.shape[1]
    return pl.pallas_call(
        _mm_pair_kernel,
        grid=(m // tm, n // tn),
        in_specs=[pl.BlockSpec((tm, k), lambda i, j: (i, 0)),
                  pl.BlockSpec((k, tn), lambda i, j: (0, j)),
                  pl.BlockSpec((k, tn), lambda i, j: (0, j))],
        out_specs=[pl.BlockSpec((tm, tn), lambda i, j: (i, j)),
                   pl.BlockSpec((tm, tn), lambda i, j: (i, j))],
        out_shape=[jax.ShapeDtypeStruct((m, n), f32), jax.ShapeDtypeStruct((m, n), bf16)],
        compiler_params=_params("parallel", "parallel"),
        name="ffn_gate_up",
    )(a, bg, bu)


def _merge(ya, zb, wa, wb, proj, ga_off, gb_off, *, tm, tn):
    m, ka = ya.shape
    kb = zb.shape[1]
    n = wa.shape[1]
    ga_blk, gb_blk = ga_off // tn, gb_off // tn
    return pl.pallas_call(
        _merge_kernel,
        grid=(n // tn, m // tm),
        in_specs=[pl.BlockSpec((tm, ka), lambda j, i: (i, 0)),
                  pl.BlockSpec((tm, kb), lambda j, i: (i, 0)),
                  pl.BlockSpec((ka, tn), lambda j, i: (0, j)),
                  pl.BlockSpec((kb, tn), lambda j, i: (0, j)),
                  pl.BlockSpec((tm, tn), lambda j, i: (i, ga_blk + j)),
                  pl.BlockSpec((tm, tn), lambda j, i: (i, gb_blk + j))],
        out_specs=pl.BlockSpec((tm, tn), lambda j, i: (i, j)),
        out_shape=jax.ShapeDtypeStruct((m, n), bf16),
        compiler_params=_params("parallel", "parallel"),
        name="gated_merge",
    )(ya, zb, wa, wb, proj, proj)


def _rope_tables(n_tok, lp):
    rows = n_tok // GRID_W
    row = jnp.repeat(jnp.arange(rows, dtype=f32), GRID_W)
    col = jnp.tile(jnp.arange(GRID_W, dtype=f32), rows)
    lead = jnp.zeros((N_META_TOKENS,), f32)
    tail = jnp.zeros((lp - N_META_TOKENS - n_tok,), f32)
    row = jnp.concatenate([lead, row, tail])
    col = jnp.concatenate([lead, col, tail])
    inv_freq = ROPE_THETA ** (-jnp.arange(ROPE_HALF, dtype=f32) * 2.0 / (2 * ROPE_HALF))
    ar, ac = row[:, None] * inv_freq, col[:, None] * inv_freq
    cr, sr, cc, sc = jnp.cos(ar), jnp.sin(ar), jnp.cos(ac), jnp.sin(ac)
    zero = jnp.zeros_like(sr)
    cos = jnp.concatenate([cr, cr, cc, cc], axis=-1)
    sin_up = jnp.concatenate([-sr, zero, -sc, zero], axis=-1)
    sin_dn = jnp.concatenate([zero, sr, zero, sc], axis=-1)
    return cos, sin_up, sin_dn


def _qk_prep_kernel(x_ref, g_ref, cos_ref, sup_ref, sdn_ref, o_ref):
    x = x_ref[...]
    y = x * lax.rsqrt(jnp.mean(x * x, axis=-1, keepdims=True) + NORM_EPS) * g_ref[...]
    out = (y * cos_ref[...]
           + pltpu.roll(y, HEAD_DIM - ROPE_HALF, 1) * sup_ref[...]
           + pltpu.roll(y, ROPE_HALF, 1) * sdn_ref[...])
    o_ref[...] = out.astype(o_ref.dtype)


def _qk_prep(proj, gains, cos, sin_up, sin_dn):
    lp = proj.shape[0]
    n_heads = N_Q_HEADS + N_KV_HEADS
    tr = _pick(lp, (512, 256, 128))
    tab = pl.BlockSpec((tr, HEAD_DIM), lambda i, h: (i, 0))
    return pl.pallas_call(
        _qk_prep_kernel,
        grid=(lp // tr, n_heads),
        in_specs=[pl.BlockSpec((tr, HEAD_DIM), lambda i, h: (i, h)),
                  pl.BlockSpec((None, 1, HEAD_DIM), lambda i, h: (h, 0, 0)),
                  tab, tab, tab],
        out_specs=pl.BlockSpec((tr, HEAD_DIM), lambda i, h: (i, h)),
        out_shape=jax.ShapeDtypeStruct((lp, n_heads * HEAD_DIM), bf16),
        compiler_params=_params("parallel", "parallel"),
        name="qk_norm_rope",
    )(proj, gains, cos, sin_up, sin_dn)


SOFTMAX_ROWS = 16


def _flash_kernel(q_ref, k_ref, v_ref, bias_ref, o_ref, m_scr, alpha_scr, acc_scr, s_scr, p_scr):
    j = pl.program_id(2)
    last = pl.num_programs(2) - 1
    tq = s_scr.shape[1]

    @pl.when(j == 0)
    def _():
        m_scr[...] = jnp.full(m_scr.shape, -jnp.inf, f32)
        acc_scr[...] = jnp.zeros(acc_scr.shape, f32)

    def scores(g):
        q = q_ref[:, g * HEAD_DIM:(g + 1) * HEAD_DIM]
        s_scr[g] = lax.dot_general(q, k_ref[...], (((1,), (1,)), ((), ())),
                                   preferred_element_type=f32)

    def softmax(g, masked):
        for r in range(0, tq, SOFTMAX_ROWS):
            rows = slice(r, r + SOFTMAX_ROWS)
            s = s_scr[g, rows, :]
            if masked:
                s = s + bias_ref[...]
            m_prev = m_scr[g, rows, :]
            m_new = jnp.maximum(m_prev, jnp.max(s, axis=-1, keepdims=True))
            p_scr[g, rows, :] = jnp.exp2(s - m_new).astype(bf16)
            alpha_scr[g, rows, :] = jnp.exp2(m_prev - m_new)
            m_scr[g, rows, :] = m_new

    def weighted_sum(g):
        acc_scr[g] = alpha_scr[g] * acc_scr[g] + _dot(p_scr[g], v_ref[...])

    def step(masked):
        scores(0)
        for g in range(GQA_GROUP):
            if g + 1 < GQA_GROUP:
                scores(g + 1)
            softmax(g, masked)
            weighted_sum(g)

    pl.when(j < last)(lambda: step(False))
    pl.when(j == last)(lambda: step(True))

    @pl.when(j == last)
    def _():
        for g in range(GQA_GROUP):
            acc = acc_scr[g]
            o_ref[:, g * HEAD_DIM:(g + 1) * HEAD_DIM] = (
                acc[:, :HEAD_DIM] / acc[:, HEAD_DIM:HEAD_DIM + 1]).astype(o_ref.dtype)


def _flash_attention(qk, v_ext, key_bias, *, tq, tk):
    lp = qk.shape[0]
    gw = GQA_GROUP * HEAD_DIM
    vw = 2 * HEAD_DIM
    return pl.pallas_call(
        _flash_kernel,
        grid=(N_KV_HEADS, lp // tq, lp // tk),
        in_specs=[pl.BlockSpec((tq, gw), lambda h, i, j: (i, h)),
                  pl.BlockSpec((tk, HEAD_DIM), lambda h, i, j: (j, N_Q_HEADS + h)),
                  pl.BlockSpec((tk, vw), lambda h, i, j: (j, h)),
                  pl.BlockSpec((1, tk), lambda h, i, j: (0, j))],
        out_specs=pl.BlockSpec((tq, gw), lambda h, i, j: (i, h)),
        out_shape=jax.ShapeDtypeStruct((lp, ATTN_WIDTH), bf16),
        scratch_shapes=[pltpu.VMEM((GQA_GROUP, tq, 1), f32),
                        pltpu.VMEM((GQA_GROUP, tq, 1), f32),
                        pltpu.VMEM((GQA_GROUP, tq, vw), f32),
                        pltpu.VMEM((GQA_GROUP, tq, tk), f32),
                        pltpu.VMEM((GQA_GROUP, tq, tk), bf16)],
        compiler_params=_params("parallel", "parallel", "arbitrary"),
        name="flash_gqa",
    )(qk, qk, v_ext, key_bias)


def _conv3_rows(x, prev, nxt, w, row0, n_valid):
    tr = x.shape[0]
    loc = lax.broadcasted_iota(jnp.int32, (tr, 1), 0)
    t = loc + row0
    xm = jnp.where(loc == 0, prev[SUBLANES - 1:SUBLANES, :], pltpu.roll(x, 1, 0))
    xm = jnp.where(t == 0, 0.0, xm)
    xp = jnp.where(loc == tr - 1, nxt[0:1, :], pltpu.roll(x, tr - 1, 0))
    xp = jnp.where(t >= n_valid - 1, 0.0, xp)
    return w[0:1, :] * xm + w[1:2, :] * x + w[2:3, :] * xp


def _halo_specs(tr, tc, n_rows, col_blk0):
    rb = tr // SUBLANES
    last = n_rows // SUBLANES - 1
    return [pl.BlockSpec((tr, tc), lambda i, j: (jnp.minimum(i, n_rows // tr - 1), col_blk0 + j)),
            pl.BlockSpec((SUBLANES, tc), lambda i, j: (jnp.clip(i * rb - 1, 0, last), col_blk0 + j)),
            pl.BlockSpec((SUBLANES, tc), lambda i, j: (jnp.minimum((i + 1) * rb, last), col_blk0 + j))]


def _hyena_conv_kernel(x_ref, p_ref, n_ref, w_ref, b_ref, o_ref, *, n_valid):
    tr = x_ref.shape[0]
    row0 = pl.program_id(0) * tr
    y = _conv3_rows(x_ref[...], p_ref[...], n_ref[...], w_ref[...], row0, n_valid) + b_ref[...]
    t = row0 + lax.broadcasted_iota(jnp.int32, (tr, 1), 0)
    o_ref[...] = jnp.where(t < n_valid, y, 0.0)


def _hyena_conv(proj, w, b, col_off, n_valid, out_rows):
    lp = proj.shape[0]
    width = w.shape[1]
    tc = _pick(math.gcd(col_off, width), (1024, 512, 256, 128))
    tr = _pick(math.gcd(lp, out_rows), (512, 256, 128))
    return pl.pallas_call(
        functools.partial(_hyena_conv_kernel, n_valid=n_valid),
        grid=(out_rows // tr, width // tc),
        in_specs=_halo_specs(tr, tc, lp, col_off // tc) + [
            pl.BlockSpec((3, tc), lambda i, j: (0, j)),
            pl.BlockSpec((1, tc), lambda i, j: (0, j))],
        out_specs=pl.BlockSpec((tr, tc), lambda i, j: (i, j)),
        out_shape=jax.ShapeDtypeStruct((out_rows, width), f32),
        compiler_params=_params("parallel", "parallel"),
        name="hyena_short_conv",
    )(proj, proj, proj, w, b.reshape(1, width))


def _ffn_act_kernel(g_ref, p_ref, n_ref, u_ref, w_ref, b_ref, o_ref, *, n_valid):
    tr = g_ref.shape[0]
    row0 = pl.program_id(0) * tr
    gate = _conv3_rows(g_ref[...], p_ref[...], n_ref[...], w_ref[...], row0, n_valid) + b_ref[...]
    o_ref[...] = (gate * jax.nn.sigmoid(gate) * u_ref[...].astype(f32)).astype(o_ref.dtype)


def _ffn_act(gate_pre, up, w, b, n_valid):
    lp, width = gate_pre.shape
    tc = _pick(width, (256, 128))
    tr = _pick(lp, (1536, 1024, 512, 256, 128))
    return pl.pallas_call(
        functools.partial(_ffn_act_kernel, n_valid=n_valid),
        grid=(lp // tr, width // tc),
        in_specs=_halo_specs(tr, tc, lp, 0) + [
            pl.BlockSpec((tr, tc), lambda i, j: (i, j)),
            pl.BlockSpec((3, tc), lambda i, j: (0, j)),
            pl.BlockSpec((1, tc), lambda i, j: (0, j))],
        out_specs=pl.BlockSpec((tr, tc), lambda i, j: (i, j)),
        out_shape=jax.ShapeDtypeStruct((lp, width), bf16),
        compiler_params=_params("parallel", "parallel"),
        name="ffn_conv_silu_gate",
    )(gate_pre, gate_pre, gate_pre, up, w, b.reshape(1, width))


EMB_COLS = LANES
COL_FWD = FILTER_EMB_DIM
COL_BWD = FILTER_EMB_DIM + 1
COL_T = FILTER_EMB_DIM + 2


def _filter_inputs(seq_len, n_fft, n_rows):
    t = jnp.linspace(0.0, 1.0, seq_len, dtype=f32)[:, None]
    w = 2.0 * math.pi * jnp.arange(seq_len, dtype=f32)[:, None] / seq_len
    f = jnp.linspace(1e-4, FILTER_BANDS - 1, FILTER_BANDS, dtype=f32)[None, :]
    emb = jnp.concatenate([t, jnp.cos(f * w), -jnp.sin(f * w)], axis=-1)
    n = jnp.arange(n_rows)
    fwd = n < seq_len
    bwd = (n > n_fft - seq_len) & (n < n_fft)
    lag = jnp.where(fwd, n, jnp.where(bwd, n_fft - n, 0))
    cols = [emb[lag], fwd.astype(f32)[:, None], bwd.astype(f32)[:, None], t[lag],
            jnp.zeros((n_rows, EMB_COLS - FILTER_EMB_DIM - 3), f32)]
    return jnp.concatenate(cols, axis=-1)


def _filter_kernel(e_ref, w1_ref, b1_ref, w2_ref, b2_ref, w3_ref, b3_ref, w4h_ref, w4l_ref, fr_ref,
                   ad_ref, o0_ref, o1_ref):
    e = e_ref[...]
    freq = fr_ref[...]

    def layer(x, w_ref, b_ref):
        xh, xl = _split(x)
        wh, wl = _split(w_ref[...])
        return jnp.sin(freq * (_dot3(xh, xl, wh, wl) + b_ref[...]))

    hid = layer(layer(layer(e, w1_ref, b1_ref), w2_ref, b2_ref), w3_ref, b3_ref)
    hh, hl = _split(hid)
    scale = (e[:, COL_FWD:COL_FWD + 1] + e[:, COL_BWD:COL_BWD + 1]) * jnp.exp(
        -e[:, COL_T:COL_T + 1] * ad_ref[...])
    o0_ref[...] = _dot3(hh, hl, w4h_ref[0], w4l_ref[0]) * scale
    o1_ref[...] = _dot3(hh, hl, w4h_ref[1], w4l_ref[1]) * scale


def _filters(e_ext, w1, b1, w2, b2, w3, b3, w4, freq, n_ch, seq_len):
    n_rows = e_ext.shape[0]
    hid = w2.shape[0]
    tr = LANES
    w1p = jnp.zeros((EMB_COLS, hid), f32).at[:FILTER_EMB_DIM].set(w1)
    w4h, w4l = _split(w4.reshape(hid, 2, 2, n_ch).transpose(1, 2, 0, 3))
    abs_delta = jnp.abs(jnp.linspace(MIN_DECAY, MAX_DECAY, n_ch, dtype=f32)).reshape(1, n_ch)
    full = lambda *shape: pl.BlockSpec(shape, lambda i: (0,) * len(shape))
    fwd_blocks = -(-seq_len // tr)
    w4_spec = pl.BlockSpec((2, None, hid, n_ch), lambda i: (0, jnp.where(i < fwd_blocks, 0, 1), 0, 0))
    out_spec = pl.BlockSpec((tr, n_ch), lambda i: (i, 0))
    return pl.pallas_call(
        _filter_kernel,
        grid=(n_rows // tr,),
        in_specs=[pl.BlockSpec((tr, EMB_COLS), lambda i: (i, 0)),
                  full(EMB_COLS, hid), full(1, hid), full(hid, hid), full(1, hid),
                  full(hid, hid), full(1, hid), w4_spec, w4_spec, full(1, hid), full(1, n_ch)],
        out_specs=[out_spec, out_spec],
        out_shape=[jax.ShapeDtypeStruct((n_rows, n_ch), f32)] * 2,
        compiler_params=_params("parallel"),
        name="hyena_filters",
    )(e_ext, w1p, b1.reshape(1, hid), w2, b2.reshape(1, hid), w3, b3.reshape(1, hid), w4h, w4l,
      freq.reshape(1, hid), abs_delta)


def _dft_tables(n1, nz):
    n_fft = n1 * LANES
    k1n = (n1 + 1) // 2
    two_pi = 2.0 * math.pi

    def angle(m, period):
        m = jnp.where(m > period // 2, m - period, m)
        return m.astype(f32) * (two_pi / period)

    k1 = jnp.arange(k1n, dtype=jnp.int32)
    nn = jnp.arange(n1, dtype=jnp.int32)
    a1 = angle((k1[:, None] * nn[None, :]) % n1, n1)
    fwd = jnp.stack([jnp.cos(a1), -jnp.sin(a1)], axis=1).reshape(2 * k1n, n1)
    wgt = jnp.where(k1 == 0, 1.0, 2.0).astype(f32) / n_fft
    inv = (jnp.stack([jnp.cos(a1), -jnp.sin(a1)], axis=1) * wgt[:, None, None])
    inv = inv.reshape(2 * k1n, n1).T[:nz]

    n2 = jnp.arange(LANES, dtype=jnp.int32)
    k2 = jnp.arange(LANES, dtype=jnp.int32)
    m = (n2[None, None, :] * (k2[None, :, None] * n1 + k1[:, None, None])) % n_fft
    a2 = angle(m, n_fft)
    gr, gi = jnp.cos(a2), -jnp.sin(a2)
    g = jnp.concatenate([jnp.concatenate([gr, -gi], axis=2),
                         jnp.concatenate([gi, gr], axis=2)], axis=1)
    grt, git = jnp.swapaxes(gr, 1, 2), jnp.swapaxes(gi, 1, 2)
    ginv = jnp.concatenate([jnp.concatenate([grt, git], axis=2),
                            jnp.concatenate([-git, grt], axis=2)], axis=1)
    return fwd, _split(g), _split(ginv), inv


def _kron8(m):
    return _split(jnp.kron(m, jnp.eye(SUBLANES, dtype=f32)))


def _resident(shape):
    return pl.BlockSpec(shape, lambda *_: (0,) * len(shape), pipeline_mode=pl.Buffered(1))


def _dft_rows_kernel(mh_ref, ml_ref, x_ref, o_ref):
    kd, sub, tw = x_ref.shape
    xh, xl = _split(x_ref[...].reshape(kd * sub, tw))
    o_ref[...] = _dot3(mh_ref[...], ml_ref[...], xh, xl).reshape(o_ref.shape)


def _dft_rows(mat8, x3, n_ch, *, slot, tw):
    mh, ml = mat8
    r, kd = mh.shape[0] // SUBLANES, mh.shape[1] // SUBLANES
    ct = n_ch // tw
    return pl.pallas_call(
        _dft_rows_kernel,
        grid=(LANES // SUBLANES, ct),
        in_specs=[_resident(mh.shape), _resident(ml.shape),
                  pl.BlockSpec((kd, SUBLANES, tw), lambda jb, c: (0, jb, slot * ct + c))],
        out_specs=pl.BlockSpec((r, SUBLANES, tw), lambda jb, c: (0, jb, c)),
        out_shape=jax.ShapeDtypeStruct((r, LANES, n_ch), f32),
        compiler_params=_params("parallel", "parallel"),
        name="dft_outer_fwd",
    )(mh, ml, x3)


def _spectrum_kernel(gh_ref, gl_ref, a_ref, o_ref):
    ah, al = _split(a_ref[...])
    o_ref[...] = _dot3(gh_ref[...], gl_ref[...], ah, al)


def _spectrum(g, a3, *, tc):
    gh, gl = g
    k1n, two_l, n_ch = a3.shape
    gspec = pl.BlockSpec((None, two_l, two_l), lambda k, c: (k, 0, 0))
    dspec = pl.BlockSpec((None, two_l, tc), lambda k, c: (k, 0, c))
    return pl.pallas_call(
        _spectrum_kernel,
        grid=(k1n, n_ch // tc),
        in_specs=[gspec, gspec, dspec],
        out_specs=dspec,
        out_shape=jax.ShapeDtypeStruct(a3.shape, f32),
        compiler_params=_params("parallel", "parallel"),
        name="dft_inner_fwd",
    )(gh, gl, a3)


def _spectral_conv_kernel(gh_ref, gl_ref, ih_ref, il_ref, a_ref, kf_ref, o_ref):
    ah, al = _split(a_ref[...])
    x = _dot3(gh_ref[...], gl_ref[...], ah, al)
    kf = kf_ref[...]
    xr, xi = x[:LANES], x[LANES:]
    kr, ki = kf[:LANES], kf[LANES:]
    y = jnp.concatenate([xr * kr - xi * ki, xr * ki + xi * kr], axis=0)
    yh, yl = _split(y)
    o_ref[...] = _dot3(ih_ref[...], il_ref[...], yh, yl)


def _spectral_conv(g, ginv, a3, kf3, *, tc):
    gh, gl = g
    ih, il = ginv
    k1n, two_l, n_ch = a3.shape
    gspec = pl.BlockSpec((None, two_l, two_l), lambda k, c: (k, 0, 0))
    dspec = pl.BlockSpec((None, two_l, tc), lambda k, c: (k, 0, c))
    return pl.pallas_call(
        _spectral_conv_kernel,
        grid=(k1n, n_ch // tc),
        in_specs=[gspec, gspec, gspec, gspec, dspec, dspec],
        out_specs=dspec,
        out_shape=jax.ShapeDtypeStruct(a3.shape, f32),
        compiler_params=_params("parallel", "parallel"),
        name="dft_inner_conv",
    )(gh, gl, ih, il, a3, kf3)


def _idft_gate_kernel(mh_ref, ml_ref, b_ref, z_ref, gate_ref, skip_ref, o_ref):
    r, sub, tw = b_ref.shape
    bh, bl = _split(b_ref[...].reshape(r * sub, tw))
    y = _dot3(mh_ref[...], ml_ref[...], bh, bl).reshape(o_ref.shape)
    o_ref[...] = (gate_ref[...] * (y + skip_ref[...] * z_ref[...])).astype(o_ref.dtype)


def _idft_gate(mat8, b3, z3, hyc3, gate_slot, skip, out_dtype, *, tw):
    mh, ml = mat8
    nz, r = mh.shape[0] // SUBLANES, mh.shape[1] // SUBLANES
    n_ch = skip.shape[1]
    ct = n_ch // tw
    tile = lambda c_of: pl.BlockSpec((nz, SUBLANES, tw), lambda jb, c: (0, jb, c_of(c)))
    return pl.pallas_call(
        _idft_gate_kernel,
        grid=(LANES // SUBLANES, ct),
        in_specs=[_resident(mh.shape), _resident(ml.shape),
                  pl.BlockSpec((r, SUBLANES, tw), lambda jb, c: (0, jb, c)),
                  tile(lambda c: c),
                  tile(lambda c: gate_slot * ct + c),
                  pl.BlockSpec((1, tw), lambda jb, c: (0, c))],
        out_specs=tile(lambda c: c),
        out_shape=jax.ShapeDtypeStruct((nz, LANES, n_ch), out_dtype),
        compiler_params=_params("parallel", "parallel"),
        name="dft_outer_inv_gate",
    )(mh, ml, b3, z3, hyc3, skip)


def kernel(x, meta_tokens, norm_mix, w_in, q_norm, k_norm, hyena_conv_w, hyena_conv_b, filt_w1, filt_b1, filt_w2, filt_b2, filt_w3, filt_b3, filt_w4, filt_freq, hyena_skip, w_attn_branch, w_hyena_branch, w_out, norm_ffn, w_ffn_gate, w_ffn_up, ffn_conv_w, ffn_conv_b, w_ffn_down, norm_final):
    batch, n_tok, d_model = x.shape
    depth = norm_mix.shape[0]
    n_ch = w_hyena_branch.shape[1]
    d_ff = w_ffn_gate.shape[2]
    seq_len = n_tok + N_META_TOKENS
    lp = _round_up(seq_len, ROW_PAD)
    nz = lp // LANES
    n1 = -(-(seq_len - 1 + _round_up(seq_len, LANES)) // LANES) | 1
    n_fft = n1 * LANES
    k1n = (n1 + 1) // 2
    hy_off = ATTN_WIDTH + 2 * KV_WIDTH
    ga_off = hy_off + 3 * n_ch
    gb_off = ga_off + d_model

    cos, sin_up, sin_dn = _rope_tables(n_tok, lp)
    key_bias = jnp.where(jnp.arange(lp) < seq_len, 0.0, MASK_BIAS).astype(f32).reshape(1, lp)
    e_ext = _filter_inputs(seq_len, n_fft, n_fft)
    fwd_f32, g_m, ginv_m, inv_f32 = _dft_tables(n1, nz)
    fwd_m, fwd_z, inv_m = _kron8(fwd_f32), _kron8(fwd_f32[:, :nz]), _kron8(inv_f32)

    tm = _pick(lp, (512, 256, 128))
    tq = _pick(lp, (512, 256, 128))
    tk = _pick(lp, (1536, 1024, 512, 256, 128))
    assert lp - tk < seq_len, "every key tile must hold at least one real token"
    tw = _pick(n_ch, (2048, 1024, 512, 256, 128))
    tw_outer = _pick(n_ch, (512, 256, 128))

    outs = []
    for b in range(batch):
        h = jnp.concatenate([meta_tokens.astype(f32), x[b],
                             jnp.zeros((lp - seq_len, d_model), f32)], axis=0)
        for i in range(depth):
            u = _rmsnorm(h, norm_mix[i], seq_len, bf16)
            in_w = w_in[i].shape[1]
            proj = _matmul(u, w_in[i].astype(bf16), f32, tm=tm,
                           tn=_pick(in_w, (1024, 512, 256, 128)), name="in_proj")

            q_gain = q_norm[i] * (HEAD_DIM ** -0.5 * math.log2(math.e))
            gains = jnp.concatenate([jnp.broadcast_to(q_gain, (N_Q_HEADS, HEAD_DIM)),
                                     jnp.broadcast_to(k_norm[i], (N_KV_HEADS, HEAD_DIM))])
            qk = _qk_prep(proj, gains.reshape(-1, 1, HEAD_DIM), cos, sin_up, sin_dn)
            v = proj[:, ATTN_WIDTH + KV_WIDTH:hy_off].astype(bf16).reshape(lp, N_KV_HEADS, HEAD_DIM)
            v_ext = jnp.concatenate([v, jnp.ones((lp, N_KV_HEADS, 1), bf16),
                                     jnp.zeros((lp, N_KV_HEADS, HEAD_DIM - 1), bf16)], axis=-1)
            y_attn = _flash_attention(qk, v_ext.reshape(lp, N_KV_HEADS * 2 * HEAD_DIM), key_bias,
                                      tq=tq, tk=tk)

            hyc = _hyena_conv(proj, hyena_conv_w[i], hyena_conv_b[i], hy_off, seq_len, lp)
            hyc3 = hyc.reshape(nz, LANES, 3 * n_ch)
            filt = _filters(e_ext, filt_w1[i], filt_b1[i], filt_w2[i], filt_b2[i], filt_w3[i],
                            filt_b3[i], filt_w4[i], filt_freq[i], n_ch, seq_len)
            z3 = hyc3
            for o in range(2):
                kf_a = _dft_rows(fwd_m, filt[o].reshape(n1, LANES, n_ch), n_ch, slot=0, tw=tw_outer)
                kf3 = _spectrum(g_m, kf_a.reshape(k1n, 2 * LANES, n_ch), tc=tw)
                za = _dft_rows(fwd_z, z3, n_ch, slot=0, tw=tw_outer)
                zb = _spectral_conv(g_m, ginv_m, za.reshape(k1n, 2 * LANES, n_ch), kf3, tc=tw)
                z3 = _idft_gate(inv_m, zb.reshape(2 * k1n, LANES, n_ch), z3, hyc3, 1 + o,
                                hyena_skip[i, o].reshape(1, n_ch), f32, tw=tw_outer)
            z = z3.reshape(lp, n_ch)

            tn_m = _pick(math.gcd(math.gcd(ga_off, gb_off), d_model), (1024, 512, 256, 128))
            mixed = _merge(y_attn, z, w_attn_branch[i].astype(bf16), w_hyena_branch[i].astype(bf16),
                           proj, ga_off, gb_off, tm=tm, tn=tn_m)
            h = _matmul(mixed, w_out[i].astype(bf16), f32, tm=tm,
                        tn=_pick(d_model, (1024, 512, 256, 128)), res=h, name="out_proj")

            u2 = _rmsnorm(h, norm_ffn[i], seq_len, bf16)
            gate_pre, up = _matmul_pair(u2, w_ffn_gate[i].astype(bf16), w_ffn_up[i].astype(bf16),
                                        tm=_pick(lp, (1536, 1024, 512, 256, 128)),
                                        tn=_pick(d_ff, (256, 128)))
            act = _ffn_act(gate_pre, up, ffn_conv_w[i], ffn_conv_b[i], seq_len)
            h = _matmul(act, w_ffn_down[i].astype(bf16), f32, tm=tm,
                        tn=_pick(d_model, (512, 256, 128)), res=h, name="ffn_down")
        out = _rmsnorm(h, norm_final, lp, f32)
        outs.append(out[None, N_META_TOKENS:seq_len])
    return outs[0] if batch == 1 else jnp.concatenate(outs, axis=0)
```

```python
import functools
import math

import jax
import jax.numpy as jnp
from jax import lax
from jax.experimental import pallas as pl
from jax.experimental.pallas import tpu as pltpu

f32 = jnp.float32
bf16 = jnp.bfloat16

N_META_TOKENS = 16
GRID_W = 64
HEAD_DIM = 128
N_Q_HEADS = 16
N_KV_HEADS = 4
GQA_GROUP = N_Q_HEADS // N_KV_HEADS
ATTN_WIDTH = N_Q_HEADS * HEAD_DIM
KV_WIDTH = N_KV_HEADS * HEAD_DIM
ROPE_THETA = 10000.0
ROPE_HALF = HEAD_DIM // 4
FILTER_EMB_DIM = 33
FILTER_BANDS = (FILTER_EMB_DIM - 1) // 2
DECAY_TARGET = 1e-2
MIN_DECAY = math.log(DECAY_TARGET) / 1.5
MAX_DECAY = math.log(DECAY_TARGET) / 0.3
NORM_EPS = 1e-6

LANES = 128
SUBLANES = 8
V7X_VMEM_LIMIT_BYTES = 56 * 1024 * 1024
ROW_PAD = 512
MASK_BIAS = -1e30


def _round_up(x, m):
    return (x + m - 1) // m * m


def _pick(n, cands):
    for c in cands:
        if n % c == 0:
            return c
    return n


def _params(*sem):
    return pltpu.CompilerParams(dimension_semantics=sem, vmem_limit_bytes=V7X_VMEM_LIMIT_BYTES)


def _split(x):
    hi = x.astype(bf16)
    lo = (x - hi.astype(f32)).astype(bf16)
    return hi, lo


def _dot(a, b):
    return jnp.dot(a, b, preferred_element_type=f32)


def _dot3(ah, al, bh, bl):
    return _dot(ah, bh) + _dot(ah, bl) + _dot(al, bh)


def _rmsnorm_kernel(x_ref, g_ref, o_ref, *, n_valid):
    tr = x_ref.shape[0]
    x = x_ref[...]
    y = x * lax.rsqrt(jnp.mean(x * x, axis=-1, keepdims=True) + NORM_EPS) * g_ref[...]
    rows = pl.program_id(0) * tr + lax.broadcasted_iota(jnp.int32, (tr, 1), 0)
    o_ref[...] = jnp.where(rows < n_valid, y, 0.0).astype(o_ref.dtype)


def _rmsnorm(x, g, n_valid, out_dtype):
    m, d = x.shape
    tr = _pick(m, (256, 128, 64, 32, 16))
    return pl.pallas_call(
        functools.partial(_rmsnorm_kernel, n_valid=n_valid),
        grid=(m // tr,),
        in_specs=[pl.BlockSpec((tr, d), lambda i: (i, 0)),
                  pl.BlockSpec((1, d), lambda i: (0, 0))],
        out_specs=pl.BlockSpec((tr, d), lambda i: (i, 0)),
        out_shape=jax.ShapeDtypeStruct((m, d), out_dtype),
        compiler_params=_params("parallel"),
        name="rmsnorm",
    )(x, g.reshape(1, d))


def _mm_kernel(a_ref, b_ref, o_ref):
    o_ref[...] = _dot(a_ref[...], b_ref[...]).astype(o_ref.dtype)


def _mm_res_kernel(a_ref, b_ref, r_ref, o_ref):
    o_ref[...] = r_ref[...] + _dot(a_ref[...], b_ref[...])


def _mm_pair_kernel(a_ref, bg_ref, bu_ref, g_ref, u_ref):
    a = a_ref[...]
    g_ref[...] = _dot(a, bg_ref[...])
    u_ref[...] = _dot(a, bu_ref[...]).astype(u_ref.dtype)


def _merge_kernel(ya_ref, zb_ref, wa_ref, wb_ref, ga_ref, gb_ref, o_ref):
    a = _dot(ya_ref[...], wa_ref[...])
    b = _dot(zb_ref[...].astype(bf16), wb_ref[...])
    o_ref[...] = (jax.nn.sigmoid(ga_ref[...]) * a + jax.nn.sigmoid(gb_ref[...]) * b).astype(o_ref.dtype)


def _matmul(a, b, out_dtype, *, tm, tn, res=None, name):
    m, k = a.shape
    n = b.shape[1]
    in_specs = [pl.BlockSpec((tm, k), lambda j, i: (i, 0)),
                pl.BlockSpec((k, tn), lambda j, i: (0, j))]
    args = [a, b]
    kern = _mm_kernel
    if res is not None:
        in_specs.append(pl.BlockSpec((tm, tn), lambda j, i: (i, j)))
        args.append(res)
        kern = _mm_res_kernel
    return pl.pallas_call(
        kern,
        grid=(n // tn, m // tm),
        in_specs=in_specs,
        out_specs=pl.BlockSpec((tm, tn), lambda j, i: (i, j)),
        out_shape=jax.ShapeDtypeStruct((m, n), out_dtype),
        compiler_params=_params("parallel", "parallel"),
        name=name,
    )(*args)


def _matmul_pair(a, bg, bu, *, tm, tn):
    m, k = a.shape
    n = bg.shape[1]
    return pl.pallas_call(
        _mm_pair_kernel,
        grid=(m // tm, n // tn),
        in_specs=[pl.BlockSpec((tm, k), lambda i, j: (i, 0)),
                  pl.BlockSpec((k, tn), lambda i, j: (0, j)),
                  pl.BlockSpec((k, tn), lambda i, j: (0, j))],
        out_specs=[pl.BlockSpec((tm, tn), lambda i, j: (i, j)),
                   pl.BlockSpec((tm, tn), lambda i, j: (i, j))],
        out_shape=[jax.ShapeDtypeStruct((m, n), f32), jax.ShapeDtypeStruct((m, n), bf16)],
        compiler_params=_params("parallel", "parallel"),
        name="ffn_gate_up",
    )(a, bg, bu)


def _merge(ya, zb, wa, wb, proj, ga_off, gb_off, *, tm, tn):
    m, ka = ya.shape
    kb = zb.shape[1]
    n = wa.shape[1]
    ga_blk, gb_blk = ga_off // tn, gb_off // tn
    return pl.pallas_call(
        _merge_kernel,
        grid=(n // tn, m // tm),
        in_specs=[pl.BlockSpec((tm, ka), lambda j, i: (i, 0)),
                  pl.BlockSpec((tm, kb), lambda j, i: (i, 0)),
                  pl.BlockSpec((ka, tn), lambda j, i: (0, j)),
                  pl.BlockSpec((kb, tn), lambda j, i: (0, j)),
                  pl.BlockSpec((tm, tn), lambda j, i: (i, ga_blk + j)),
                  pl.BlockSpec((tm, tn), lambda j, i: (i, gb_blk + j))],
        out_specs=pl.BlockSpec((tm, tn), lambda j, i: (i, j)),
        out_shape=jax.ShapeDtypeStruct((m, n), bf16),
        compiler_params=_params("parallel", "parallel"),
        name="gated_merge",
    )(ya, zb, wa, wb, proj, proj)


def _rope_tables(n_tok, lp):
    rows = n_tok // GRID_W
    row = jnp.repeat(jnp.arange(rows, dtype=f32), GRID_W)
    col = jnp.tile(jnp.arange(GRID_W, dtype=f32), rows)
    lead = jnp.zeros((N_META_TOKENS,), f32)
    tail = jnp.zeros((lp - N_META_TOKENS - n_tok,), f32)
    row = jnp.concatenate([lead, row, tail])
    col = jnp.concatenate([lead, col, tail])
    inv_freq = ROPE_THETA ** (-jnp.arange(ROPE_HALF, dtype=f32) * 2.0 / (2 * ROPE_HALF))
    ar, ac = row[:, None] * inv_freq, col[:, None] * inv_freq
    cr, sr, cc, sc = jnp.cos(ar), jnp.sin(ar), jnp.cos(ac), jnp.sin(ac)
    zero = jnp.zeros_like(sr)
    cos = jnp.concatenate([cr, cr, cc, cc], axis=-1)
    sin_up = jnp.concatenate([-sr, zero, -sc, zero], axis=-1)
    sin_dn = jnp.concatenate([zero, sr, zero, sc], axis=-1)
    return cos, sin_up, sin_dn


def _qk_prep_kernel(x_ref, g_ref, cos_ref, sup_ref, sdn_ref, o_ref):
    cos, sup, sdn = cos_ref[...], sup_ref[...], sdn_ref[...]
    for h in range(N_Q_HEADS + N_KV_HEADS):
        cols = slice(h * HEAD_DIM, (h + 1) * HEAD_DIM)
        x = x_ref[:, cols]
        y = x * lax.rsqrt(jnp.mean(x * x, axis=-1, keepdims=True) + NORM_EPS) * g_ref[h]
        out = (y * cos + pltpu.roll(y, HEAD_DIM - ROPE_HALF, 1) * sup
               + pltpu.roll(y, ROPE_HALF, 1) * sdn)
        o_ref[:, cols] = out.astype(o_ref.dtype)


def _qk_prep(proj, gains, cos, sin_up, sin_dn):
    lp = proj.shape[0]
    width = (N_Q_HEADS + N_KV_HEADS) * HEAD_DIM
    tr = _pick(lp, (256, 128))
    tab = pl.BlockSpec((tr, HEAD_DIM), lambda i: (i, 0))
    return pl.pallas_call(
        _qk_prep_kernel,
        grid=(lp // tr,),
        in_specs=[pl.BlockSpec((tr, width), lambda i: (i, 0)),
                  pl.BlockSpec(gains.shape, lambda i: (0, 0, 0)),
                  tab, tab, tab],
        out_specs=pl.BlockSpec((tr, width), lambda i: (i, 0)),
        out_shape=jax.ShapeDtypeStruct((lp, width), bf16),
        compiler_params=_params("parallel"),
        name="qk_norm_rope",
    )(proj, gains, cos, sin_up, sin_dn)


SOFTMAX_ROWS = 16
FLASH_UNIT_ROWS = 256


def _flash_kernel(q_ref, k_ref, v_ref, bias_ref, o_ref, m_scr, alpha_scr, acc_scr, s_scr, p_scr):
    j = pl.program_id(2)
    last = pl.num_programs(2) - 1
    tq = s_scr.shape[1]

    @pl.when(j == 0)
    def _():
        m_scr[...] = jnp.full(m_scr.shape, -jnp.inf, f32)
        acc_scr[...] = jnp.zeros(acc_scr.shape, f32)

    unit_rows = min(tq, FLASH_UNIT_ROWS)
    units = [(g, slice(r, r + unit_rows)) for g in range(GQA_GROUP) for r in range(0, tq, unit_rows)]

    def scores(u, masked):
        g, rows = u
        q = q_ref[rows, g * HEAD_DIM:(g + 1) * HEAD_DIM]
        s = lax.dot_general(q, k_ref[...], (((1,), (1,)), ((), ())), preferred_element_type=f32)
        if masked:
            s = s + bias_ref[...]
        s_scr[g, rows, :] = s
        m_prev = m_scr[g, rows, :]
        m_new = jnp.maximum(m_prev, jnp.max(s, axis=-1, keepdims=True))
        alpha_scr[g, rows, :] = jnp.exp2(m_prev - m_new)
        m_scr[g, rows, :] = m_new

    def softmax(u):
        g, unit = u
        for r in range(unit.start, unit.stop, SOFTMAX_ROWS):
            rows = slice(r, r + SOFTMAX_ROWS)
            p_scr[g, rows, :] = jnp.exp2(s_scr[g, rows, :] - m_scr[g, rows, :]).astype(bf16)

    def weighted_sum(u):
        g, rows = u
        acc_scr[g, rows, :] = (alpha_scr[g, rows, :] * acc_scr[g, rows, :]
                               + _dot(p_scr[g, rows, :], v_ref[...]))

    def step(masked):
        scores(units[0], masked)
        for i, u in enumerate(units):
            if i + 1 < len(units):
                scores(units[i + 1], masked)
            softmax(u)
            weighted_sum(u)

    pl.when(j < last)(lambda: step(False))
    pl.when(j == last)(lambda: step(True))

    @pl.when(j == last)
    def _():
        for g in range(GQA_GROUP):
            acc = acc_scr[g]
            o_ref[:, g * HEAD_DIM:(g + 1) * HEAD_DIM] = (
                acc[:, :HEAD_DIM] / acc[:, HEAD_DIM:HEAD_DIM + 1]).astype(o_ref.dtype)


def _flash_attention(qk, v_ext, key_bias, *, tq, tk):
    lp = qk.shape[0]
    gw = GQA_GROUP * HEAD_DIM
    vw = 2 * HEAD_DIM
    return pl.pallas_call(
        _flash_kernel,
        grid=(N_KV_HEADS, lp // tq, lp // tk),
        in_specs=[pl.BlockSpec((tq, gw), lambda h, i, j: (i, h)),
                  pl.BlockSpec((tk, HEAD_DIM), lambda h, i, j: (j, N_Q_HEADS + h)),
                  pl.BlockSpec((tk, vw), lambda h, i, j: (j, h)),
                  pl.BlockSpec((1, tk), lambda h, i, j: (0, j))],
        out_specs=pl.BlockSpec((tq, gw), lambda h, i, j: (i, h)),
        out_shape=jax.ShapeDtypeStruct((lp, ATTN_WIDTH), bf16),
        scratch_shapes=[pltpu.VMEM((GQA_GROUP, tq, 1), f32),
                        pltpu.VMEM((GQA_GROUP, tq, 1), f32),
                        pltpu.VMEM((GQA_GROUP, tq, vw), f32),
                        pltpu.VMEM((GQA_GROUP, tq, tk), f32),
                        pltpu.VMEM((GQA_GROUP, tq, tk), bf16)],
        compiler_params=_params("parallel", "parallel", "arbitrary"),
        name="flash_gqa",
    )(qk, qk, v_ext, key_bias)


def _conv3_rows(x, prev, nxt, w, row0, n_valid):
    tr = x.shape[0]
    loc = lax.broadcasted_iota(jnp.int32, (tr, 1), 0)
    t = loc + row0
    xm = jnp.where(loc == 0, prev[SUBLANES - 1:SUBLANES, :], pltpu.roll(x, 1, 0))
    xm = jnp.where(t == 0, 0.0, xm)
    xp = jnp.where(loc == tr - 1, nxt[0:1, :], pltpu.roll(x, tr - 1, 0))
    xp = jnp.where(t >= n_valid - 1, 0.0, xp)
    return w[0:1, :] * xm + w[1:2, :] * x + w[2:3, :] * xp


def _halo_specs(tr, tc, n_rows, col_blk0):
    rb = tr // SUBLANES
    last = n_rows // SUBLANES - 1
    return [pl.BlockSpec((tr, tc), lambda i, j: (jnp.minimum(i, n_rows // tr - 1), col_blk0 + j)),
            pl.BlockSpec((SUBLANES, tc), lambda i, j: (jnp.clip(i * rb - 1, 0, last), col_blk0 + j)),
            pl.BlockSpec((SUBLANES, tc), lambda i, j: (jnp.minimum((i + 1) * rb, last), col_blk0 + j))]


def _hyena_conv_kernel(x_ref, p_ref, n_ref, w_ref, b_ref, o_ref, *, n_valid):
    tr = x_ref.shape[0]
    row0 = pl.program_id(0) * tr
    y = _conv3_rows(x_ref[...], p_ref[...], n_ref[...], w_ref[...], row0, n_valid) + b_ref[...]
    t = row0 + lax.broadcasted_iota(jnp.int32, (tr, 1), 0)
    o_ref[...] = jnp.where(t < n_valid, y, 0.0)


def _hyena_conv(proj, w, b, col_off, n_valid, out_rows):
    lp = proj.shape[0]
    width = w.shape[1]
    tc = _pick(math.gcd(col_off, width), (1024, 512, 256, 128))
    tr = _pick(math.gcd(lp, out_rows), (512, 256, 128))
    return pl.pallas_call(
        functools.partial(_hyena_conv_kernel, n_valid=n_valid),
        grid=(out_rows // tr, width // tc),
        in_specs=_halo_specs(tr, tc, lp, col_off // tc) + [
            pl.BlockSpec((3, tc), lambda i, j: (0, j)),
            pl.BlockSpec((1, tc), lambda i, j: (0, j))],
        out_specs=pl.BlockSpec((tr, tc), lambda i, j: (i, j)),
        out_shape=jax.ShapeDtypeStruct((out_rows, width), f32),
        compiler_params=_params("parallel", "parallel"),
        name="hyena_short_conv",
    )(proj, proj, proj, w, b.reshape(1, width))


def _ffn_act_kernel(g_ref, p_ref, n_ref, u_ref, w_ref, b_ref, o_ref, *, n_valid):
    tr = g_ref.shape[0]
    row0 = pl.program_id(0) * tr
    gate = _conv3_rows(g_ref[...], p_ref[...], n_ref[...], w_ref[...], row0, n_valid) + b_ref[...]
    o_ref[...] = (gate * jax.nn.sigmoid(gate) * u_ref[...].astype(f32)).astype(o_ref.dtype)


def _ffn_act(gate_pre, up, w, b, n_valid):
    lp, width = gate_pre.shape
    tc = _pick(width, (256, 128))
    tr = _pick(lp, (1536, 1024, 512, 256, 128))
    return pl.pallas_call(
        functools.partial(_ffn_act_kernel, n_valid=n_valid),
        grid=(lp // tr, width // tc),
        in_specs=_halo_specs(tr, tc, lp, 0) + [
            pl.BlockSpec((tr, tc), lambda i, j: (i, j)),
            pl.BlockSpec((3, tc), lambda i, j: (0, j)),
            pl.BlockSpec((1, tc), lambda i, j: (0, j))],
        out_specs=pl.BlockSpec((tr, tc), lambda i, j: (i, j)),
        out_shape=jax.ShapeDtypeStruct((lp, width), bf16),
        compiler_params=_params("parallel", "parallel"),
        name="ffn_conv_silu_gate",
    )(gate_pre, gate_pre, gate_pre, up, w, b.reshape(1, width))


EMB_COLS = LANES
COL_FWD = FILTER_EMB_DIM
COL_BWD = FILTER_EMB_DIM + 1
COL_T = FILTER_EMB_DIM + 2


def _filter_inputs(seq_len, n_fft, n_rows):
    t = jnp.linspace(0.0, 1.0, seq_len, dtype=f32)[:, None]
    w = 2.0 * math.pi * jnp.arange(seq_len, dtype=f32)[:, None] / seq_len
    f = jnp.linspace(1e-4, FILTER_BANDS - 1, FILTER_BANDS, dtype=f32)[None, :]
    emb = jnp.concatenate([t, jnp.cos(f * w), -jnp.sin(f * w)], axis=-1)
    n = jnp.arange(n_rows)
    fwd = n < seq_len
    bwd = (n > n_fft - seq_len) & (n < n_fft)
    lag = jnp.where(fwd, n, jnp.where(bwd, n_fft - n, 0))
    cols = [emb[lag], fwd.astype(f32)[:, None], bwd.astype(f32)[:, None], t[lag],
            jnp.zeros((n_rows, EMB_COLS - FILTER_EMB_DIM - 3), f32)]
    return jnp.concatenate(cols, axis=-1)


def _filter_kernel(e_ref, w1_ref, b1_ref, w2_ref, b2_ref, w3_ref, b3_ref, w4h_ref, w4l_ref, fr_ref,
                   ad_ref, o0_ref, o1_ref):
    e = e_ref[...]
    freq = fr_ref[...]

    def layer(x, w_ref, b_ref):
        xh, xl = _split(x)
        wh, wl = _split(w_ref[...])
        return jnp.sin(freq * (_dot3(xh, xl, wh, wl) + b_ref[...]))

    hid = layer(layer(layer(e, w1_ref, b1_ref), w2_ref, b2_ref), w3_ref, b3_ref)
    hh, hl = _split(hid)
    scale = (e[:, COL_FWD:COL_FWD + 1] + e[:, COL_BWD:COL_BWD + 1]) * jnp.exp(
        -e[:, COL_T:COL_T + 1] * ad_ref[...])
    o0_ref[...] = _dot3(hh, hl, w4h_ref[0], w4l_ref[0]) * scale
    o1_ref[...] = _dot3(hh, hl, w4h_ref[1], w4l_ref[1]) * scale


def _filters(e_ext, w1, b1, w2, b2, w3, b3, w4, freq, n_ch, seq_len):
    n_rows = e_ext.shape[0]
    hid = w2.shape[0]
    tr = LANES
    w1p = jnp.zeros((EMB_COLS, hid), f32).at[:FILTER_EMB_DIM].set(w1)
    w4h, w4l = _split(w4.reshape(hid, 2, 2, n_ch).transpose(1, 2, 0, 3))
    abs_delta = jnp.abs(jnp.linspace(MIN_DECAY, MAX_DECAY, n_ch, dtype=f32)).reshape(1, n_ch)
    full = lambda *shape: pl.BlockSpec(shape, lambda i: (0,) * len(shape))
    fwd_blocks = -(-seq_len // tr)
    w4_spec = pl.BlockSpec((2, None, hid, n_ch), lambda i: (0, jnp.where(i < fwd_blocks, 0, 1), 0, 0))
    out_spec = pl.BlockSpec((tr, n_ch), lambda i: (i, 0))
    return pl.pallas_call(
        _filter_kernel,
        grid=(n_rows // tr,),
        in_specs=[pl.BlockSpec((tr, EMB_COLS), lambda i: (i, 0)),
                  full(EMB_COLS, hid), full(1, hid), full(hid, hid), full(1, hid),
                  full(hid, hid), full(1, hid), w4_spec, w4_spec, full(1, hid), full(1, n_ch)],
        out_specs=[out_spec, out_spec],
        out_shape=[jax.ShapeDtypeStruct((n_rows, n_ch), f32)] * 2,
        compiler_params=_params("parallel"),
        name="hyena_filters",
    )(e_ext, w1p, b1.reshape(1, hid), w2, b2.reshape(1, hid), w3, b3.reshape(1, hid), w4h, w4l,
      freq.reshape(1, hid), abs_delta)


def _dft_tables(n1, nz):
    n_fft = n1 * LANES
    k1n = (n1 + 1) // 2
    two_pi = 2.0 * math.pi

    def angle(m, period):
        m = jnp.where(m > period // 2, m - period, m)
        return m.astype(f32) * (two_pi / period)

    k1 = jnp.arange(k1n, dtype=jnp.int32)
    nn = jnp.arange(n1, dtype=jnp.int32)
    a1 = angle((k1[:, None] * nn[None, :]) % n1, n1)
    fwd = jnp.stack([jnp.cos(a1), -jnp.sin(a1)], axis=1).reshape(2 * k1n, n1)
    wgt = jnp.where(k1 == 0, 1.0, 2.0).astype(f32) / n_fft
    inv = (jnp.stack([jnp.cos(a1), -jnp.sin(a1)], axis=1) * wgt[:, None, None])
    inv = inv.reshape(2 * k1n, n1).T[:nz]

    n2 = jnp.arange(LANES, dtype=jnp.int32)
    k2 = jnp.arange(LANES, dtype=jnp.int32)
    m = (n2[None, None, :] * (k2[None, :, None] * n1 + k1[:, None, None])) % n_fft
    a2 = angle(m, n_fft)
    gr, gi = jnp.cos(a2), -jnp.sin(a2)
    g = jnp.concatenate([jnp.concatenate([gr, -gi], axis=2),
                         jnp.concatenate([gi, gr], axis=2)], axis=1)
    grt, git = jnp.swapaxes(gr, 1, 2), jnp.swapaxes(gi, 1, 2)
    ginv = jnp.concatenate([jnp.concatenate([grt, git], axis=2),
                            jnp.concatenate([-git, grt], axis=2)], axis=1)
    return fwd, g.astype(bf16), ginv.astype(bf16), inv


def _kron8(m):
    return jnp.kron(m, jnp.eye(SUBLANES, dtype=f32)).astype(bf16)


def _resident(shape):
    return pl.BlockSpec(shape, lambda *_: (0,) * len(shape), pipeline_mode=pl.Buffered(1))


def _dft_rows_kernel(m_ref, x_ref, o_ref):
    kd, sub, tw = x_ref.shape
    x = x_ref[...].reshape(kd * sub, tw).astype(bf16)
    o_ref[...] = _dot(m_ref[...], x).reshape(o_ref.shape)


def _dft_rows(mat8, x3, n_ch, *, slot, tw):
    r, kd = mat8.shape[0] // SUBLANES, mat8.shape[1] // SUBLANES
    ct = n_ch // tw
    return pl.pallas_call(
        _dft_rows_kernel,
        grid=(LANES // SUBLANES, ct),
        in_specs=[_resident(mat8.shape),
                  pl.BlockSpec((kd, SUBLANES, tw), lambda jb, c: (0, jb, slot * ct + c))],
        out_specs=pl.BlockSpec((r, SUBLANES, tw), lambda jb, c: (0, jb, c)),
        out_shape=jax.ShapeDtypeStruct((r, LANES, n_ch), f32),
        compiler_params=_params("parallel", "parallel"),
        name="dft_outer_fwd",
    )(mat8, x3)


def _spectrum_kernel(g_ref, a_ref, o_ref):
    o_ref[...] = _dot(g_ref[...], a_ref[...].astype(bf16))


def _spectrum(g, a3, *, tc):
    k1n, two_l, n_ch = a3.shape
    gspec = pl.BlockSpec((None, two_l, two_l), lambda k, c: (k, 0, 0))
    dspec = pl.BlockSpec((None, two_l, tc), lambda k, c: (k, 0, c))
    return pl.pallas_call(
        _spectrum_kernel,
        grid=(k1n, n_ch // tc),
        in_specs=[gspec, dspec],
        out_specs=dspec,
        out_shape=jax.ShapeDtypeStruct(a3.shape, f32),
        compiler_params=_params("parallel", "parallel"),
        name="dft_inner_fwd",
    )(g, a3)


def _spectral_conv_kernel(g_ref, ginv_ref, a_ref, kf_ref, o_ref):
    x = _dot(g_ref[...], a_ref[...].astype(bf16))
    kf = kf_ref[...]
    xr, xi = x[:LANES], x[LANES:]
    kr, ki = kf[:LANES], kf[LANES:]
    y = jnp.concatenate([xr * kr - xi * ki, xr * ki + xi * kr], axis=0)
    o_ref[...] = _dot(ginv_ref[...], y.astype(bf16))


def _spectral_conv(g, ginv, a3, kf3, *, tc):
    k1n, two_l, n_ch = a3.shape
    gspec = pl.BlockSpec((None, two_l, two_l), lambda k, c: (k, 0, 0))
    dspec = pl.BlockSpec((None, two_l, tc), lambda k, c: (k, 0, c))
    return pl.pallas_call(
        _spectral_conv_kernel,
        grid=(k1n, n_ch // tc),
        in_specs=[gspec, gspec, dspec, dspec],
        out_specs=dspec,
        out_shape=jax.ShapeDtypeStruct(a3.shape, f32),
        compiler_params=_params("parallel", "parallel"),
        name="dft_inner_conv",
    )(g, ginv, a3, kf3)


def _idft_gate_kernel(m_ref, b_ref, z_ref, gate_ref, skip_ref, o_ref):
    r, sub, tw = b_ref.shape
    y = _dot(m_ref[...], b_ref[...].reshape(r * sub, tw).astype(bf16)).reshape(o_ref.shape)
    o_ref[...] = (gate_ref[...] * (y + skip_ref[...] * z_ref[...])).astype(o_ref.dtype)


def _idft_gate(mat8, b3, z3, hyc3, gate_slot, skip, out_dtype, *, tw):
    nz, r = mat8.shape[0] // SUBLANES, mat8.shape[1] // SUBLANES
    n_ch = skip.shape[1]
    ct = n_ch // tw
    tile = lambda c_of: pl.BlockSpec((nz, SUBLANES, tw), lambda jb, c: (0, jb, c_of(c)))
    return pl.pallas_call(
        _idft_gate_kernel,
        grid=(LANES // SUBLANES, ct),
        in_specs=[_resident(mat8.shape),
                  pl.BlockSpec((r, SUBLANES, tw), lambda jb, c: (0, jb, c)),
                  tile(lambda c: c),
                  tile(lambda c: gate_slot * ct + c),
                  pl.BlockSpec((1, tw), lambda jb, c: (0, c))],
        out_specs=tile(lambda c: c),
        out_shape=jax.ShapeDtypeStruct((nz, LANES, n_ch), out_dtype),
        compiler_params=_params("parallel", "parallel"),
        name="dft_outer_inv_gate",
    )(mat8, b3, z3, hyc3, skip)


def kernel(x, meta_tokens, norm_mix, w_in, q_norm, k_norm, hyena_conv_w, hyena_conv_b, filt_w1, filt_b1, filt_w2, filt_b2, filt_w3, filt_b3, filt_w4, filt_freq, hyena_skip, w_attn_branch, w_hyena_branch, w_out, norm_ffn, w_ffn_gate, w_ffn_up, ffn_conv_w, ffn_conv_b, w_ffn_down, norm_final):
    batch, n_tok, d_model = x.shape
    depth = norm_mix.shape[0]
    n_ch = w_hyena_branch.shape[1]
    d_ff = w_ffn_gate.shape[2]
    seq_len = n_tok + N_META_TOKENS
    lp = _round_up(seq_len, ROW_PAD)
    nz = lp // LANES
    n1 = -(-(seq_len - 1 + _round_up(seq_len, LANES)) // LANES) | 1
    n_fft = n1 * LANES
    k1n = (n1 + 1) // 2
    hy_off = ATTN_WIDTH + 2 * KV_WIDTH
    ga_off = hy_off + 3 * n_ch
    gb_off = ga_off + d_model

    cos, sin_up, sin_dn = _rope_tables(n_tok, lp)
    key_bias = jnp.where(jnp.arange(lp) < seq_len, 0.0, MASK_BIAS).astype(f32).reshape(1, lp)
    e_ext = _filter_inputs(seq_len, n_fft, n_fft)
    fwd_f32, g_m, ginv_m, inv_f32 = _dft_tables(n1, nz)
    fwd_m, fwd_z, inv_m = _kron8(fwd_f32), _kron8(fwd_f32[:, :nz]), _kron8(inv_f32)

    tm = _pick(lp, (512, 256, 128))
    tq = _pick(lp, (512, 256, 128))
    tk = _pick(lp, (1536, 1024, 512, 256, 128))
    assert lp - tk < seq_len, "every key tile must hold at least one real token"
    tw = _pick(n_ch, (2048, 1024, 512, 256, 128))
    tw_outer = _pick(n_ch, (512, 256, 128))

    outs = []
    for b in range(batch):
        h = jnp.concatenate([meta_tokens.astype(f32), x[b],
                             jnp.zeros((lp - seq_len, d_model), f32)], axis=0)
        for i in range(depth):
            u = _rmsnorm(h, norm_mix[i], seq_len, bf16)
            in_w = w_in[i].shape[1]
            proj = _matmul(u, w_in[i].astype(bf16), f32, tm=tm,
                           tn=_pick(in_w, (1024, 512, 256, 128)), name="in_proj")

            q_gain = q_norm[i] * (HEAD_DIM ** -0.5 * math.log2(math.e))
            gains = jnp.concatenate([jnp.broadcast_to(q_gain, (N_Q_HEADS, HEAD_DIM)),
                                     jnp.broadcast_to(k_norm[i], (N_KV_HEADS, HEAD_DIM))])
            qk = _qk_prep(proj, gains.reshape(-1, 1, HEAD_DIM), cos, sin_up, sin_dn)
            v = proj[:, ATTN_WIDTH + KV_WIDTH:hy_off].astype(bf16).reshape(lp, N_KV_HEADS, HEAD_DIM)
            v_ext = jnp.concatenate([v, jnp.ones((lp, N_KV_HEADS, 1), bf16),
                                     jnp.zeros((lp, N_KV_HEADS, HEAD_DIM - 1), bf16)], axis=-1)
            y_attn = _flash_attention(qk, v_ext.reshape(lp, N_KV_HEADS * 2 * HEAD_DIM), key_bias,
                                      tq=tq, tk=tk)

            hyc = _hyena_conv(proj, hyena_conv_w[i], hyena_conv_b[i], hy_off, seq_len, lp)
            hyc3 = hyc.reshape(nz, LANES, 3 * n_ch)
            filt = _filters(e_ext, filt_w1[i], filt_b1[i], filt_w2[i], filt_b2[i], filt_w3[i],
                            filt_b3[i], filt_w4[i], filt_freq[i], n_ch, seq_len)
            z3 = hyc3
            for o in range(2):
                kf_a = _dft_rows(fwd_m, filt[o].reshape(n1, LANES, n_ch), n_ch, slot=0, tw=tw_outer)
                kf3 = _spectrum(g_m, kf_a.reshape(k1n, 2 * LANES, n_ch), tc=tw)
                za = _dft_rows(fwd_z, z3, n_ch, slot=0, tw=tw_outer)
                zb = _spectral_conv(g_m, ginv_m, za.reshape(k1n, 2 * LANES, n_ch), kf3, tc=tw)
                z3 = _idft_gate(inv_m, zb.reshape(2 * k1n, LANES, n_ch), z3, hyc3, 1 + o,
                                hyena_skip[i, o].reshape(1, n_ch), f32, tw=tw_outer)
            z = z3.reshape(lp, n_ch)

            tn_m = _pick(math.gcd(math.gcd(ga_off, gb_off), d_model), (1024, 512, 256, 128))
            mixed = _merge(y_attn, z, w_attn_branch[i].astype(bf16), w_hyena_branch[i].astype(bf16),
                           proj, ga_off, gb_off, tm=tm, tn=tn_m)
            h = _matmul(mixed, w_out[i].astype(bf16), f32, tm=tm,
                        tn=_pick(d_model, (1024, 512, 256, 128)), res=h, name="out_proj")

            u2 = _rmsnorm(h, norm_ffn[i], seq_len, bf16)
            gate_pre, up = _matmul_pair(u2, w_ffn_gate[i].astype(bf16), w_ffn_up[i].astype(bf16),
                                        tm=_pick(lp, (1536, 1024, 512, 256, 128)),
                                        tn=_pick(d_ff, (256, 128)))
            act = _ffn_act(gate_pre, up, ffn_conv_w[i], ffn_conv_b[i], seq_len)
            h = _matmul(act, w_ffn_down[i].astype(bf16), f32, tm=tm,
                        tn=_pick(d_model, (512, 256, 128)), res=h, name="ffn_down")
        out = _rmsnorm(h, norm_final, lp, f32)
        outs.append(out[None, N_META_TOKENS:seq_len])
    return outs[0] if batch == 1 else jnp.concatenate(outs, axis=0)
```

```python
import functools
import math

import jax
import jax.numpy as jnp
from jax import lax
from jax.experimental import pallas as pl
from jax.experimental.pallas import tpu as pltpu

f32 = jnp.float32
bf16 = jnp.bfloat16

N_META_TOKENS = 16
GRID_W = 64
HEAD_DIM = 128
N_Q_HEADS = 16
N_KV_HEADS = 4
GQA_GROUP = N_Q_HEADS // N_KV_HEADS
ATTN_WIDTH = N_Q_HEADS * HEAD_DIM
KV_WIDTH = N_KV_HEADS * HEAD_DIM
ROPE_THETA = 10000.0
ROPE_HALF = HEAD_DIM // 4
FILTER_EMB_DIM = 33
FILTER_BANDS = (FILTER_EMB_DIM - 1) // 2
DECAY_TARGET = 1e-2
MIN_DECAY = math.log(DECAY_TARGET) / 1.5
MAX_DECAY = math.log(DECAY_TARGET) / 0.3
NORM_EPS = 1e-6

LANES = 128
SUBLANES = 8
V7X_VMEM_LIMIT_BYTES = 56 * 1024 * 1024
ROW_PAD = 512
MASK_BIAS = -1e30


def _round_up(x, m):
    return (x + m - 1) // m * m


def _pick(n, cands):
    for c in cands:
        if n % c == 0:
            return c
    return n


def _params(*sem):
    return pltpu.CompilerParams(dimension_semantics=sem, vmem_limit_bytes=V7X_VMEM_LIMIT_BYTES)


def _split(x):
    hi = x.astype(bf16)
    lo = (x - hi.astype(f32)).astype(bf16)
    return hi, lo


def _dot(a, b):
    return jnp.dot(a, b, preferred_element_type=f32)


def _dot3(ah, al, bh, bl):
    return _dot(ah, bh) + _dot(ah, bl) + _dot(al, bh)


def _rmsnorm_kernel(x_ref, g_ref, o_ref, *, n_valid):
    tr = x_ref.shape[0]
    x = x_ref[...]
    y = x * lax.rsqrt(jnp.mean(x * x, axis=-1, keepdims=True) + NORM_EPS) * g_ref[...]
    rows = pl.program_id(0) * tr + lax.broadcasted_iota(jnp.int32, (tr, 1), 0)
    o_ref[...] = jnp.where(rows < n_valid, y, 0.0).astype(o_ref.dtype)


def _rmsnorm(x, g, n_valid, out_dtype):
    m, d = x.shape
    tr = _pick(m, (256, 128, 64, 32, 16))
    return pl.pallas_call(
        functools.partial(_rmsnorm_kernel, n_valid=n_valid),
        grid=(m // tr,),
        in_specs=[pl.BlockSpec((tr, d), lambda i: (i, 0)),
                  pl.BlockSpec((1, d), lambda i: (0, 0))],
        out_specs=pl.BlockSpec((tr, d), lambda i: (i, 0)),
        out_shape=jax.ShapeDtypeStruct((m, d), out_dtype),
        compiler_params=_params("parallel"),
        name="rmsnorm",
    )(x, g.reshape(1, d))


def _mm_kernel(a_ref, b_ref, o_ref):
    o_ref[...] = _dot(a_ref[...], b_ref[...]).astype(o_ref.dtype)


def _mm_res_kernel(a_ref, b_ref, r_ref, o_ref):
    o_ref[...] = r_ref[...] + _dot(a_ref[...], b_ref[...])


FFN_HALO = 16


def _ffn_gate_kernel(a_ref, ap_ref, an_ref, wg_ref, wu_ref, cw_ref, cb_ref, o_ref, lhs_scr, *, n_valid):
    i, j = pl.program_id(0), pl.program_id(1)
    tm = a_ref.shape[0]
    ext = tm + 2 * FFN_HALO

    @pl.when(j == 0)
    def _():
        lhs_scr[0:FFN_HALO, :] = ap_ref[...]
        lhs_scr[FFN_HALO:FFN_HALO + tm, :] = a_ref[...]
        lhs_scr[FFN_HALO + tm:ext, :] = an_ref[...]

    g_ext = _dot(lhs_scr[...], wg_ref[...])
    up = _dot(lhs_scr[FFN_HALO:FFN_HALO + tm, :], wu_ref[...])
    core = slice(FFN_HALO, FFN_HALO + tm)
    t = i * tm + lax.broadcasted_iota(jnp.int32, (tm, 1), 0)
    g_prev = jnp.where(t == 0, 0.0, pltpu.roll(g_ext, 1, 0)[core])
    g_next = jnp.where(t >= n_valid - 1, 0.0, pltpu.roll(g_ext, ext - 1, 0)[core])
    cw = cw_ref[...]
    gate = cw[0:1, :] * g_prev + cw[1:2, :] * g_ext[core] + cw[2:3, :] * g_next + cb_ref[...]
    o_ref[...] = (gate * jax.nn.sigmoid(gate) * up).astype(o_ref.dtype)


def _merge_kernel(ya_ref, zb_ref, wa_ref, wb_ref, ga_ref, gb_ref, o_ref):
    a = _dot(ya_ref[...], wa_ref[...])
    b = _dot(zb_ref[...].astype(bf16), wb_ref[...])
    o_ref[...] = (jax.nn.sigmoid(ga_ref[...]) * a + jax.nn.sigmoid(gb_ref[...]) * b).astype(o_ref.dtype)


def _matmul(a, b, out_dtype, *, tm, tn, res=None, name):
    m, k = a.shape
    n = b.shape[1]
    in_specs = [pl.BlockSpec((tm, k), lambda j, i: (i, 0)),
                pl.BlockSpec((k, tn), lambda j, i: (0, j))]
    args = [a, b]
    kern = _mm_kernel
    if res is not None:
        in_specs.append(pl.BlockSpec((tm, tn), lambda j, i: (i, j)))
        args.append(res)
        kern = _mm_res_kernel
    return pl.pallas_call(
        kern,
        grid=(n // tn, m // tm),
        in_specs=in_specs,
        out_specs=pl.BlockSpec((tm, tn), lambda j, i: (i, j)),
        out_shape=jax.ShapeDtypeStruct((m, n), out_dtype),
        compiler_params=_params("parallel", "parallel"),
        name=name,
    )(*args)


def _ffn_gate(a, wg, wu, conv_w, conv_b, n_valid, *, tm, tn):
    m, k = a.shape
    n = wg.shape[1]
    hb = tm // FFN_HALO
    last = m // FFN_HALO - 1
    return pl.pallas_call(
        functools.partial(_ffn_gate_kernel, n_valid=n_valid),
        grid=(m // tm, n // tn),
        in_specs=[pl.BlockSpec((tm, k), lambda i, j: (i, 0), pipeline_mode=pl.Buffered(1)),
                  pl.BlockSpec((FFN_HALO, k), lambda i, j: (jnp.maximum(i * hb - 1, 0), 0)),
                  pl.BlockSpec((FFN_HALO, k), lambda i, j: (jnp.minimum((i + 1) * hb, last), 0)),
                  pl.BlockSpec((k, tn), lambda i, j: (0, j)),
                  pl.BlockSpec((k, tn), lambda i, j: (0, j)),
                  pl.BlockSpec((3, tn), lambda i, j: (0, j)),
                  pl.BlockSpec((1, tn), lambda i, j: (0, j))],
        out_specs=pl.BlockSpec((tm, tn), lambda i, j: (i, j)),
        out_shape=jax.ShapeDtypeStruct((m, n), bf16),
        scratch_shapes=[pltpu.VMEM((tm + 2 * FFN_HALO, k), bf16)],
        compiler_params=_params("parallel", "arbitrary"),
        name="ffn_gate_conv_act",
    )(a, a, a, wg, wu, conv_w, conv_b.reshape(1, n))


def _merge(ya, zb, wa, wb, proj, ga_off, gb_off, *, tm, tn):
    m, ka = ya.shape
    kb = zb.shape[1]
    n = wa.shape[1]
    ga_blk, gb_blk = ga_off // tn, gb_off // tn
    return pl.pallas_call(
        _merge_kernel,
        grid=(n // tn, m // tm),
        in_specs=[pl.BlockSpec((tm, ka), lambda j, i: (i, 0)),
                  pl.BlockSpec((tm, kb), lambda j, i: (i, 0)),
                  pl.BlockSpec((ka, tn), lambda j, i: (0, j)),
                  pl.BlockSpec((kb, tn), lambda j, i: (0, j)),
                  pl.BlockSpec((tm, tn), lambda j, i: (i, ga_blk + j)),
                  pl.BlockSpec((tm, tn), lambda j, i: (i, gb_blk + j))],
        out_specs=pl.BlockSpec((tm, tn), lambda j, i: (i, j)),
        out_shape=jax.ShapeDtypeStruct((m, n), bf16),
        compiler_params=_params("parallel", "parallel"),
        name="gated_merge",
    )(ya, zb, wa, wb, proj, proj)


def _rope_tables(n_tok, lp):
    rows = n_tok // GRID_W
    row = jnp.repeat(jnp.arange(rows, dtype=f32), GRID_W)
    col = jnp.tile(jnp.arange(GRID_W, dtype=f32), rows)
    lead = jnp.zeros((N_META_TOKENS,), f32)
    tail = jnp.zeros((lp - N_META_TOKENS - n_tok,), f32)
    row = jnp.concatenate([lead, row, tail])
    col = jnp.concatenate([lead, col, tail])
    inv_freq = ROPE_THETA ** (-jnp.arange(ROPE_HALF, dtype=f32) * 2.0 / (2 * ROPE_HALF))
    ar, ac = row[:, None] * inv_freq, col[:, None] * inv_freq
    cr, sr, cc, sc = jnp.cos(ar), jnp.sin(ar), jnp.cos(ac), jnp.sin(ac)
    zero = jnp.zeros_like(sr)
    cos = jnp.concatenate([cr, cr, cc, cc], axis=-1)
    sin_up = jnp.concatenate([-sr, zero, -sc, zero], axis=-1)
    sin_dn = jnp.concatenate([zero, sr, zero, sc], axis=-1)
    return cos, sin_up, sin_dn


def _qk_prep_kernel(x_ref, g_ref, cos_ref, sup_ref, sdn_ref, o_ref):
    cos, sup, sdn = cos_ref[...], sup_ref[...], sdn_ref[...]
    for h in range(N_Q_HEADS + N_KV_HEADS):
        cols = slice(h * HEAD_DIM, (h + 1) * HEAD_DIM)
        x = x_ref[:, cols]
        y = x * lax.rsqrt(jnp.mean(x * x, axis=-1, keepdims=True) + NORM_EPS) * g_ref[h]
        out = (y * cos + pltpu.roll(y, HEAD_DIM - ROPE_HALF, 1) * sup
               + pltpu.roll(y, ROPE_HALF, 1) * sdn)
        o_ref[:, cols] = out.astype(o_ref.dtype)


def _qk_prep(proj, gains, cos, sin_up, sin_dn):
    lp = proj.shape[0]
    width = (N_Q_HEADS + N_KV_HEADS) * HEAD_DIM
    tr = _pick(lp, (256, 128))
    tab = pl.BlockSpec((tr, HEAD_DIM), lambda i: (i, 0))
    return pl.pallas_call(
        _qk_prep_kernel,
        grid=(lp // tr,),
        in_specs=[pl.BlockSpec((tr, width), lambda i: (i, 0)),
                  pl.BlockSpec(gains.shape, lambda i: (0, 0, 0)),
                  tab, tab, tab],
        out_specs=pl.BlockSpec((tr, width), lambda i: (i, 0)),
        out_shape=jax.ShapeDtypeStruct((lp, width), bf16),
        compiler_params=_params("parallel"),
        name="qk_norm_rope",
    )(proj, gains, cos, sin_up, sin_dn)


SOFTMAX_ROWS = 16
FLASH_UNIT_ROWS = 128


def _flash_kernel(q_ref, k_ref, v_ref, bias_ref, o_ref, m_scr, alpha_scr, acc_scr, s_scr, p_scr):
    j = pl.program_id(2)
    last = pl.num_programs(2) - 1
    tq = s_scr.shape[1]

    @pl.when(j == 0)
    def _():
        m_scr[...] = jnp.full(m_scr.shape, -jnp.inf, f32)
        acc_scr[...] = jnp.zeros(acc_scr.shape, f32)

    unit_rows = min(tq, FLASH_UNIT_ROWS)
    units = [(g, slice(r, r + unit_rows)) for g in range(GQA_GROUP) for r in range(0, tq, unit_rows)]

    def scores(u, masked):
        g, rows = u
        q = q_ref[rows, g * HEAD_DIM:(g + 1) * HEAD_DIM]
        s = lax.dot_general(q, k_ref[...], (((1,), (1,)), ((), ())), preferred_element_type=f32)
        if masked:
            s = s + bias_ref[...]
        s_scr[g, rows, :] = s
        m_prev = m_scr[g, rows, :]
        m_new = jnp.maximum(m_prev, jnp.max(s, axis=-1, keepdims=True))
        alpha_scr[g, rows, :] = jnp.exp2(m_prev - m_new)
        m_scr[g, rows, :] = m_new

    def softmax(u):
        g, unit = u
        for r in range(unit.start, unit.stop, SOFTMAX_ROWS):
            rows = slice(r, r + SOFTMAX_ROWS)
            p_scr[g, rows, :] = jnp.exp2(s_scr[g, rows, :] - m_scr[g, rows, :]).astype(bf16)

    def weighted_sum(u):
        g, rows = u
        acc_scr[g, rows, :] = (alpha_scr[g, rows, :] * acc_scr[g, rows, :]
                               + _dot(p_scr[g, rows, :], v_ref[...]))

    def step(masked):
        scores(units[0], masked)
        for i, u in enumerate(units):
            if i + 1 < len(units):
                scores(units[i + 1], masked)
            softmax(u)
            weighted_sum(u)

    pl.when(j < last)(lambda: step(False))
    pl.when(j == last)(lambda: step(True))

    @pl.when(j == last)
    def _():
        for g in range(GQA_GROUP):
            acc = acc_scr[g]
            o_ref[:, g * HEAD_DIM:(g + 1) * HEAD_DIM] = (
                acc[:, :HEAD_DIM] / acc[:, HEAD_DIM:HEAD_DIM + 1]).astype(o_ref.dtype)


def _flash_attention(qk, v_ext, key_bias, *, tq, tk):
    lp = qk.shape[0]
    gw = GQA_GROUP * HEAD_DIM
    vw = 2 * HEAD_DIM
    return pl.pallas_call(
        _flash_kernel,
        grid=(N_KV_HEADS, lp // tq, lp // tk),
        in_specs=[pl.BlockSpec((tq, gw), lambda h, i, j: (i, h)),
                  pl.BlockSpec((tk, HEAD_DIM), lambda h, i, j: (j, N_Q_HEADS + h)),
                  pl.BlockSpec((tk, vw), lambda h, i, j: (j, h)),
                  pl.BlockSpec((1, tk), lambda h, i, j: (0, j))],
        out_specs=pl.BlockSpec((tq, gw), lambda h, i, j: (i, h)),
        out_shape=jax.ShapeDtypeStruct((lp, ATTN_WIDTH), bf16),
        scratch_shapes=[pltpu.VMEM((GQA_GROUP, tq, 1), f32),
                        pltpu.VMEM((GQA_GROUP, tq, 1), f32),
                        pltpu.VMEM((GQA_GROUP, tq, vw), f32),
                        pltpu.VMEM((GQA_GROUP, tq, tk), f32),
                        pltpu.VMEM((GQA_GROUP, tq, tk), bf16)],
        compiler_params=_params("parallel", "parallel", "arbitrary"),
        name="flash_gqa",
    )(qk, qk, v_ext, key_bias)


def _conv3_rows(x, prev, nxt, w, row0, n_valid):
    tr = x.shape[0]
    loc = lax.broadcasted_iota(jnp.int32, (tr, 1), 0)
    t = loc + row0
    xm = jnp.where(loc == 0, prev[SUBLANES - 1:SUBLANES, :], pltpu.roll(x, 1, 0))
    xm = jnp.where(t == 0, 0.0, xm)
    xp = jnp.where(loc == tr - 1, nxt[0:1, :], pltpu.roll(x, tr - 1, 0))
    xp = jnp.where(t >= n_valid - 1, 0.0, xp)
    return w[0:1, :] * xm + w[1:2, :] * x + w[2:3, :] * xp


def _halo_specs(tr, tc, n_rows, col_blk0):
    rb = tr // SUBLANES
    last = n_rows // SUBLANES - 1
    return [pl.BlockSpec((tr, tc), lambda i, j: (jnp.minimum(i, n_rows // tr - 1), col_blk0 + j)),
            pl.BlockSpec((SUBLANES, tc), lambda i, j: (jnp.clip(i * rb - 1, 0, last), col_blk0 + j)),
            pl.BlockSpec((SUBLANES, tc), lambda i, j: (jnp.minimum((i + 1) * rb, last), col_blk0 + j))]


def _hyena_conv_kernel(x_ref, p_ref, n_ref, w_ref, b_ref, o_ref, *, n_valid):
    tr = x_ref.shape[0]
    row0 = pl.program_id(0) * tr
    y = _conv3_rows(x_ref[...], p_ref[...], n_ref[...], w_ref[...], row0, n_valid) + b_ref[...]
    t = row0 + lax.broadcasted_iota(jnp.int32, (tr, 1), 0)
    o_ref[...] = jnp.where(t < n_valid, y, 0.0)


def _hyena_conv(proj, w, b, col_off, n_valid, out_rows):
    lp = proj.shape[0]
    width = w.shape[1]
    tc = _pick(math.gcd(col_off, width), (1024, 512, 256, 128))
    tr = _pick(math.gcd(lp, out_rows), (512, 256, 128))
    return pl.pallas_call(
        functools.partial(_hyena_conv_kernel, n_valid=n_valid),
        grid=(out_rows // tr, width // tc),
        in_specs=_halo_specs(tr, tc, lp, col_off // tc) + [
            pl.BlockSpec((3, tc), lambda i, j: (0, j)),
            pl.BlockSpec((1, tc), lambda i, j: (0, j))],
        out_specs=pl.BlockSpec((tr, tc), lambda i, j: (i, j)),
        out_shape=jax.ShapeDtypeStruct((out_rows, width), f32),
        compiler_params=_params("parallel", "parallel"),
        name="hyena_short_conv",
    )(proj, proj, proj, w, b.reshape(1, width))


EMB_COLS = LANES
COL_FWD = FILTER_EMB_DIM
COL_BWD = FILTER_EMB_DIM + 1
COL_T = FILTER_EMB_DIM + 2


def _filter_inputs(seq_len, n_fft, n_rows):
    n = jnp.arange(n_rows)
    fwd = n < seq_len
    bwd = (n > n_fft - seq_len) & (n < n_fft)
    lag = jnp.where(fwd, n, jnp.where(bwd, n_fft - n, 0)).astype(f32)[:, None]
    t = lag / (seq_len - 1)
    w = 2.0 * math.pi * lag / seq_len
    f = jnp.linspace(1e-4, FILTER_BANDS - 1, FILTER_BANDS, dtype=f32)[None, :]
    cols = [t, jnp.cos(f * w), -jnp.sin(f * w), fwd.astype(f32)[:, None], bwd.astype(f32)[:, None], t,
            jnp.zeros((n_rows, EMB_COLS - FILTER_EMB_DIM - 3), f32)]
    return jnp.concatenate(cols, axis=-1)


def _filter_kernel(e_ref, w1_ref, b1_ref, w2_ref, b2_ref, w3_ref, b3_ref, w4h_ref, w4l_ref, fr_ref,
                   ad_ref, o0_ref, o1_ref):
    e = e_ref[...]
    freq = fr_ref[...]

    def layer(x, w_ref, b_ref):
        xh, xl = _split(x)
        wh, wl = _split(w_ref[...])
        return jnp.sin(freq * (_dot3(xh, xl, wh, wl) + b_ref[...]))

    hid = layer(layer(layer(e, w1_ref, b1_ref), w2_ref, b2_ref), w3_ref, b3_ref)
    hh, hl = _split(hid)
    scale = (e[:, COL_FWD:COL_FWD + 1] + e[:, COL_BWD:COL_BWD + 1]) * jnp.exp(
        -e[:, COL_T:COL_T + 1] * ad_ref[...])
    o0_ref[...] = _dot3(hh, hl, w4h_ref[0], w4l_ref[0]) * scale
    o1_ref[...] = _dot3(hh, hl, w4h_ref[1], w4l_ref[1]) * scale


def _filters(e_ext, w1, b1, w2, b2, w3, b3, w4, freq, n_ch, seq_len):
    n_rows = e_ext.shape[0]
    hid = w2.shape[0]
    tr = LANES
    w1p = jnp.zeros((EMB_COLS, hid), f32).at[:FILTER_EMB_DIM].set(w1)
    w4h, w4l = _split(w4.reshape(hid, 2, 2, n_ch).transpose(1, 2, 0, 3))
    abs_delta = jnp.abs(jnp.linspace(MIN_DECAY, MAX_DECAY, n_ch, dtype=f32)).reshape(1, n_ch)
    full = lambda *shape: pl.BlockSpec(shape, lambda i: (0,) * len(shape))
    fwd_blocks = -(-seq_len // tr)
    w4_spec = pl.BlockSpec((2, None, hid, n_ch), lambda i: (0, jnp.where(i < fwd_blocks, 0, 1), 0, 0))
    out_spec = pl.BlockSpec((tr, n_ch), lambda i: (i, 0))
    return pl.pallas_call(
        _filter_kernel,
        grid=(n_rows // tr,),
        in_specs=[pl.BlockSpec((tr, EMB_COLS), lambda i: (i, 0)),
                  full(EMB_COLS, hid), full(1, hid), full(hid, hid), full(1, hid),
                  full(hid, hid), full(1, hid), w4_spec, w4_spec, full(1, hid), full(1, n_ch)],
        out_specs=[out_spec, out_spec],
        out_shape=[jax.ShapeDtypeStruct((n_rows, n_ch), f32)] * 2,
        compiler_params=_params("parallel"),
        name="hyena_filters",
    )(e_ext, w1p, b1.reshape(1, hid), w2, b2.reshape(1, hid), w3, b3.reshape(1, hid), w4h, w4l,
      freq.reshape(1, hid), abs_delta)


def _dft_tables(n1, nz):
    n_fft = n1 * LANES
    k1n = (n1 + 1) // 2
    two_pi = 2.0 * math.pi

    def angle(m, period):
        m = jnp.where(m > period // 2, m - period, m)
        return m.astype(f32) * (two_pi / period)

    row = jnp.arange(2 * k1n * SUBLANES, dtype=jnp.int32)[:, None]
    col = jnp.arange(n1 * SUBLANES, dtype=jnp.int32)[None, :]
    k1r, part = row // (2 * SUBLANES), (row // SUBLANES) % 2
    a1 = angle((k1r * (col // SUBLANES)) % n1, n1)
    fwd = jnp.where(row % SUBLANES == col % SUBLANES,
                    jnp.where(part == 0, jnp.cos(a1), -jnp.sin(a1)), 0.0)
    wgt = jnp.where(k1r == 0, 1.0, 2.0).astype(f32) / n_fft
    inv = (fwd * wgt)[:, :nz * SUBLANES].T
    fwd, inv = fwd.astype(bf16), inv.astype(bf16)

    k1 = jnp.arange(k1n, dtype=jnp.int32)
    n2 = jnp.arange(LANES, dtype=jnp.int32)
    k2 = jnp.arange(LANES, dtype=jnp.int32)
    m = (n2[None, None, :] * (k2[None, :, None] * n1 + k1[:, None, None])) % n_fft
    a2 = angle(m, n_fft)
    gr, gi = jnp.cos(a2), -jnp.sin(a2)
    g = jnp.concatenate([jnp.concatenate([gr, -gi], axis=2),
                         jnp.concatenate([gi, gr], axis=2)], axis=1)
    grt, git = jnp.swapaxes(gr, 1, 2), jnp.swapaxes(gi, 1, 2)
    ginv = jnp.concatenate([jnp.concatenate([grt, git], axis=2),
                            jnp.concatenate([-git, grt], axis=2)], axis=1)
    return fwd, g.astype(bf16), ginv.astype(bf16), inv


def _resident(shape):
    return pl.BlockSpec(shape, lambda *_: (0,) * len(shape), pipeline_mode=pl.Buffered(1))


def _dft_rows_kernel(m_ref, x_ref, o_ref):
    kd, sub, tw = x_ref.shape
    x = x_ref[...].reshape(kd * sub, tw).astype(bf16)
    o_ref[...] = _dot(m_ref[...], x).reshape(o_ref.shape)


def _dft_rows(mat8, x3, n_ch, *, slot, tw):
    r, kd = mat8.shape[0] // SUBLANES, mat8.shape[1] // SUBLANES
    ct = n_ch // tw
    return pl.pallas_call(
        _dft_rows_kernel,
        grid=(LANES // SUBLANES, ct),
        in_specs=[_resident(mat8.shape),
                  pl.BlockSpec((kd, SUBLANES, tw), lambda jb, c: (0, jb, slot * ct + c))],
        out_specs=pl.BlockSpec((r, SUBLANES, tw), lambda jb, c: (0, jb, c)),
        out_shape=jax.ShapeDtypeStruct((r, LANES, n_ch), f32),
        compiler_params=_params("parallel", "parallel"),
        name="dft_outer_fwd",
    )(mat8, x3)


def _spectrum_kernel(g_ref, a_ref, o_ref):
    o_ref[...] = _dot(g_ref[...], a_ref[...].astype(bf16))


def _spectrum(g, a3, *, tc):
    k1n, two_l, n_ch = a3.shape
    gspec = pl.BlockSpec((None, two_l, two_l), lambda k, c: (k, 0, 0))
    dspec = pl.BlockSpec((None, two_l, tc), lambda k, c: (k, 0, c))
    return pl.pallas_call(
        _spectrum_kernel,
        grid=(k1n, n_ch // tc),
        in_specs=[gspec, dspec],
        out_specs=dspec,
        out_shape=jax.ShapeDtypeStruct(a3.shape, f32),
        compiler_params=_params("parallel", "parallel"),
        name="dft_inner_fwd",
    )(g, a3)


def _spectral_conv_kernel(g_ref, ginv_ref, a_ref, kf_ref, o_ref):
    x = _dot(g_ref[...], a_ref[...].astype(bf16))
    kf = kf_ref[...]
    xr, xi = x[:LANES], x[LANES:]
    kr, ki = kf[:LANES], kf[LANES:]
    y = jnp.concatenate([xr * kr - xi * ki, xr * ki + xi * kr], axis=0)
    o_ref[...] = _dot(ginv_ref[...], y.astype(bf16))


def _spectral_conv(g, ginv, a3, kf3, *, tc):
    k1n, two_l, n_ch = a3.shape
    gspec = pl.BlockSpec((None, two_l, two_l), lambda k, c: (k, 0, 0))
    dspec = pl.BlockSpec((None, two_l, tc), lambda k, c: (k, 0, c))
    return pl.pallas_call(
        _spectral_conv_kernel,
        grid=(k1n, n_ch // tc),
        in_specs=[gspec, gspec, dspec, dspec],
        out_specs=dspec,
        out_shape=jax.ShapeDtypeStruct(a3.shape, f32),
        compiler_params=_params("parallel", "parallel"),
        name="dft_inner_conv",
    )(g, ginv, a3, kf3)


def _idft_gate_kernel(m_ref, b_ref, z_ref, gate_ref, skip_ref, o_ref):
    r, sub, tw = b_ref.shape
    y = _dot(m_ref[...], b_ref[...].reshape(r * sub, tw).astype(bf16)).reshape(o_ref.shape)
    o_ref[...] = (gate_ref[...] * (y + skip_ref[...] * z_ref[...])).astype(o_ref.dtype)


def _idft_gate(mat8, b3, z3, hyc3, gate_slot, skip, out_dtype, *, tw):
    nz, r = mat8.shape[0] // SUBLANES, mat8.shape[1] // SUBLANES
    n_ch = skip.shape[1]
    ct = n_ch // tw
    tile = lambda c_of: pl.BlockSpec((nz, SUBLANES, tw), lambda jb, c: (0, jb, c_of(c)))
    return pl.pallas_call(
        _idft_gate_kernel,
        grid=(LANES // SUBLANES, ct),
        in_specs=[_resident(mat8.shape),
                  pl.BlockSpec((r, SUBLANES, tw), lambda jb, c: (0, jb, c)),
                  tile(lambda c: c),
                  tile(lambda c: gate_slot * ct + c),
                  pl.BlockSpec((1, tw), lambda jb, c: (0, c))],
        out_specs=tile(lambda c: c),
        out_shape=jax.ShapeDtypeStruct((nz, LANES, n_ch), out_dtype),
        compiler_params=_params("parallel", "parallel"),
        name="dft_outer_inv_gate",
    )(mat8, b3, z3, hyc3, skip)


def kernel(x, meta_tokens, norm_mix, w_in, q_norm, k_norm, hyena_conv_w, hyena_conv_b, filt_w1, filt_b1, filt_w2, filt_b2, filt_w3, filt_b3, filt_w4, filt_freq, hyena_skip, w_attn_branch, w_hyena_branch, w_out, norm_ffn, w_ffn_gate, w_ffn_up, ffn_conv_w, ffn_conv_b, w_ffn_down, norm_final):
    batch, n_tok, d_model = x.shape
    depth = norm_mix.shape[0]
    n_ch = w_hyena_branch.shape[1]
    d_ff = w_ffn_gate.shape[2]
    seq_len = n_tok + N_META_TOKENS
    lp = _round_up(seq_len, ROW_PAD)
    nz = lp // LANES
    n1 = -(-(seq_len - 1 + _round_up(seq_len, LANES)) // LANES) | 1
    n_fft = n1 * LANES
    k1n = (n1 + 1) // 2
    hy_off = ATTN_WIDTH + 2 * KV_WIDTH
    ga_off = hy_off + 3 * n_ch
    gb_off = ga_off + d_model

    cos, sin_up, sin_dn = _rope_tables(n_tok, lp)
    key_bias = jnp.where(jnp.arange(lp) < seq_len, 0.0, MASK_BIAS).astype(f32).reshape(1, lp)
    e_ext = _filter_inputs(seq_len, n_fft, n_fft)
    fwd_m, g_m, ginv_m, inv_m = _dft_tables(n1, nz)
    fwd_z = fwd_m[:, :nz * SUBLANES]

    tm = _pick(lp, (512, 256, 128))
    tq = _pick(lp, (512, 256, 128))
    tk = _pick(lp, (1536, 1024, 512, 256, 128))
    assert lp - tk < seq_len, "every key tile must hold at least one real token"
    tw = _pick(n_ch, (2048, 1024, 512, 256, 128))
    tw_outer = _pick(n_ch, (512, 256, 128))

    outs = []
    for b in range(batch):
        h = jnp.concatenate([meta_tokens.astype(f32), x[b],
                             jnp.zeros((lp - seq_len, d_model), f32)], axis=0)
        for i in range(depth):
            u = _rmsnorm(h, norm_mix[i], seq_len, bf16)
            in_w = w_in[i].shape[1]
            proj = _matmul(u, w_in[i].astype(bf16), f32, tm=tm,
                           tn=_pick(in_w, (1024, 512, 256, 128)), name="in_proj")

            q_gain = q_norm[i] * (HEAD_DIM ** -0.5 * math.log2(math.e))
            gains = jnp.concatenate([jnp.broadcast_to(q_gain, (N_Q_HEADS, HEAD_DIM)),
                                     jnp.broadcast_to(k_norm[i], (N_KV_HEADS, HEAD_DIM))])
            qk = _qk_prep(proj, gains.reshape(-1, 1, HEAD_DIM), cos, sin_up, sin_dn)
            v = proj[:, ATTN_WIDTH + KV_WIDTH:hy_off].astype(bf16).reshape(lp, N_KV_HEADS, HEAD_DIM)
            v_ext = jnp.concatenate([v, jnp.ones((lp, N_KV_HEADS, 1), bf16),
                                     jnp.zeros((lp, N_KV_HEADS, HEAD_DIM - 1), bf16)], axis=-1)
            y_attn = _flash_attention(qk, v_ext.reshape(lp, N_KV_HEADS * 2 * HEAD_DIM), key_bias,
                                      tq=tq, tk=tk)

            hyc = _hyena_conv(proj, hyena_conv_w[i], hyena_conv_b[i], hy_off, seq_len, lp)
            hyc3 = hyc.reshape(nz, LANES, 3 * n_ch)
            filt = _filters(e_ext, filt_w1[i], filt_b1[i], filt_w2[i], filt_b2[i], filt_w3[i],
                            filt_b3[i], filt_w4[i], filt_freq[i], n_ch, seq_len)
            z3 = hyc3
            for o in range(2):
                kf_a = _dft_rows(fwd_m, filt[o].reshape(n1, LANES, n_ch), n_ch, slot=0, tw=tw_outer)
                kf3 = _spectrum(g_m, kf_a.reshape(k1n, 2 * LANES, n_ch), tc=tw)
                za = _dft_rows(fwd_z, z3, n_ch, slot=0, tw=tw_outer)
                zb = _spectral_conv(g_m, ginv_m, za.reshape(k1n, 2 * LANES, n_ch), kf3, tc=tw)
                z3 = _idft_gate(inv_m, zb.reshape(2 * k1n, LANES, n_ch), z3, hyc3, 1 + o,
                                hyena_skip[i, o].reshape(1, n_ch), f32, tw=tw_outer)
            z = z3.reshape(lp, n_ch)

            tn_m = _pick(math.gcd(math.gcd(ga_off, gb_off), d_model), (1024, 512, 256, 128))
            mixed = _merge(y_attn, z, w_attn_branch[i].astype(bf16), w_hyena_branch[i].astype(bf16),
                           proj, ga_off, gb_off, tm=tm, tn=tn_m)
            h = _matmul(mixed, w_out[i].astype(bf16), f32, tm=tm,
                        tn=_pick(d_model, (1024, 512, 256, 128)), res=h, name="out_proj")

            u2 = _rmsnorm(h, norm_ffn[i], seq_len, bf16)
            act = _ffn_gate(u2, w_ffn_gate[i].astype(bf16), w_ffn_up[i].astype(bf16),
                            ffn_conv_w[i], ffn_conv_b[i], seq_len,
                            tm=_pick(lp, (1536, 1024, 512, 256, 128)), tn=_pick(d_ff, (256, 128)))
            h = _matmul(act, w_ffn_down[i].astype(bf16), f32, tm=tm,
                        tn=_pick(d_model, (512, 256, 128)), res=h, name="ffn_down")
        out = _rmsnorm(h, norm_final, lp, f32)
        outs.append(out[None, N_META_TOKENS:seq_len])
    return outs[0] if batch == 1 else jnp.concatenate(outs, axis=0)
```

```python
import functools
import math

import jax
import jax.numpy as jnp
from jax import lax
from jax.experimental import pallas as pl
from jax.experimental.pallas import tpu as pltpu

f32 = jnp.float32
bf16 = jnp.bfloat16

N_META_TOKENS = 16
GRID_W = 64
HEAD_DIM = 128
N_Q_HEADS = 16
N_KV_HEADS = 4
GQA_GROUP = N_Q_HEADS // N_KV_HEADS
ATTN_WIDTH = N_Q_HEADS * HEAD_DIM
KV_WIDTH = N_KV_HEADS * HEAD_DIM
ROPE_THETA = 10000.0
ROPE_HALF = HEAD_DIM // 4
FILTER_EMB_DIM = 33
FILTER_BANDS = (FILTER_EMB_DIM - 1) // 2
DECAY_TARGET = 1e-2
MIN_DECAY = math.log(DECAY_TARGET) / 1.5
MAX_DECAY = math.log(DECAY_TARGET) / 0.3
NORM_EPS = 1e-6

LANES = 128
SUBLANES = 8
V7X_VMEM_LIMIT_BYTES = 56 * 1024 * 1024
ROW_PAD = 512
MASK_BIAS = -1e30


def _round_up(x, m):
    return (x + m - 1) // m * m


def _pick(n, cands):
    for c in cands:
        if n % c == 0:
            return c
    return n


def _params(*sem):
    return pltpu.CompilerParams(dimension_semantics=sem, vmem_limit_bytes=V7X_VMEM_LIMIT_BYTES)


def _split(x):
    hi = x.astype(bf16)
    lo = (x - hi.astype(f32)).astype(bf16)
    return hi, lo


def _dot(a, b):
    return jnp.dot(a, b, preferred_element_type=f32)


def _dot3(ah, al, bh, bl):
    return _dot(ah, bh) + _dot(ah, bl) + _dot(al, bh)


def _rmsnorm_kernel(x_ref, g_ref, o_ref, *, n_valid):
    tr = x_ref.shape[0]
    x = x_ref[...]
    y = x * lax.rsqrt(jnp.mean(x * x, axis=-1, keepdims=True) + NORM_EPS) * g_ref[...]
    rows = pl.program_id(0) * tr + lax.broadcasted_iota(jnp.int32, (tr, 1), 0)
    o_ref[...] = jnp.where(rows < n_valid, y, 0.0).astype(o_ref.dtype)


def _rmsnorm(x, g, n_valid, out_dtype):
    m, d = x.shape
    tr = _pick(m, (256, 128, 64, 32, 16))
    return pl.pallas_call(
        functools.partial(_rmsnorm_kernel, n_valid=n_valid),
        grid=(m // tr,),
        in_specs=[pl.BlockSpec((tr, d), lambda i: (i, 0)),
                  pl.BlockSpec((1, d), lambda i: (0, 0))],
        out_specs=pl.BlockSpec((tr, d), lambda i: (i, 0)),
        out_shape=jax.ShapeDtypeStruct((m, d), out_dtype),
        compiler_params=_params("parallel"),
        name="rmsnorm",
    )(x, g.reshape(1, d))


def _final_norm_kernel(x_ref, nxt_ref, g_ref, o_ref):
    tr = x_ref.shape[0]

    def norm(x):
        return x * lax.rsqrt(jnp.mean(x * x, axis=-1, keepdims=True) + NORM_EPS) * g_ref[...]

    o_ref[0:tr - N_META_TOKENS, :] = norm(x_ref[N_META_TOKENS:tr, :])
    o_ref[tr - N_META_TOKENS:tr, :] = norm(nxt_ref[...])


def _final_norm(h, g, n_tok):
    d = h.shape[1]
    tr = _pick(n_tok, (256, 128))
    assert n_tok % tr == 0 and tr % N_META_TOKENS == 0 and n_tok + N_META_TOKENS <= h.shape[0]
    nb = tr // N_META_TOKENS
    return pl.pallas_call(
        _final_norm_kernel,
        grid=(n_tok // tr,),
        in_specs=[pl.BlockSpec((tr, d), lambda i: (i, 0)),
                  pl.BlockSpec((N_META_TOKENS, d), lambda i: ((i + 1) * nb, 0)),
                  pl.BlockSpec((1, d), lambda i: (0, 0))],
        out_specs=pl.BlockSpec((tr, d), lambda i: (i, 0)),
        out_shape=jax.ShapeDtypeStruct((n_tok, d), h.dtype),
        compiler_params=_params("parallel"),
        name="final_rmsnorm",
    )(h, h, g.reshape(1, d))


def _mm_kernel(a_ref, b_ref, o_ref):
    o_ref[...] = _dot(a_ref[...], b_ref[...]).astype(o_ref.dtype)


def _mm_res_kernel(a_ref, b_ref, r_ref, o_ref):
    o_ref[...] = r_ref[...] + _dot(a_ref[...], b_ref[...])


FFN_HALO = 16


def _ffn_gate_kernel(a_ref, ap_ref, an_ref, wg_ref, wu_ref, cw_ref, cb_ref, o_ref, lhs_scr, *, n_valid):
    i, j = pl.program_id(0), pl.program_id(1)
    tm = a_ref.shape[0]
    ext = tm + 2 * FFN_HALO

    @pl.when(j == 0)
    def _():
        lhs_scr[0:FFN_HALO, :] = ap_ref[...]
        lhs_scr[FFN_HALO:FFN_HALO + tm, :] = a_ref[...]
        lhs_scr[FFN_HALO + tm:ext, :] = an_ref[...]

    g_ext = _dot(lhs_scr[...], wg_ref[...])
    up = _dot(lhs_scr[FFN_HALO:FFN_HALO + tm, :], wu_ref[...])
    core = slice(FFN_HALO, FFN_HALO + tm)
    t = i * tm + lax.broadcasted_iota(jnp.int32, (tm, 1), 0)
    g_prev = jnp.where(t == 0, 0.0, pltpu.roll(g_ext, 1, 0)[core])
    g_next = jnp.where(t >= n_valid - 1, 0.0, pltpu.roll(g_ext, ext - 1, 0)[core])
    cw = cw_ref[...]
    gate = cw[0:1, :] * g_prev + cw[1:2, :] * g_ext[core] + cw[2:3, :] * g_next + cb_ref[...]
    o_ref[...] = (gate * jax.nn.sigmoid(gate) * up).astype(o_ref.dtype)


def _merge_kernel(ya_ref, zb_ref, wa_ref, wb_ref, ga_ref, gb_ref, o_ref):
    a = _dot(ya_ref[...], wa_ref[...])
    b = _dot(zb_ref[...].astype(bf16), wb_ref[...])
    o_ref[...] = (jax.nn.sigmoid(ga_ref[...]) * a + jax.nn.sigmoid(gb_ref[...]) * b).astype(o_ref.dtype)


def _matmul(a, b, out_dtype, *, tm, tn, res=None, name):
    m, k = a.shape
    n = b.shape[1]
    in_specs = [pl.BlockSpec((tm, k), lambda j, i: (i, 0)),
                pl.BlockSpec((k, tn), lambda j, i: (0, j))]
    args = [a, b]
    kern = _mm_kernel
    if res is not None:
        in_specs.append(pl.BlockSpec((tm, tn), lambda j, i: (i, j)))
        args.append(res)
        kern = _mm_res_kernel
    return pl.pallas_call(
        kern,
        grid=(n // tn, m // tm),
        in_specs=in_specs,
        out_specs=pl.BlockSpec((tm, tn), lambda j, i: (i, j)),
        out_shape=jax.ShapeDtypeStruct((m, n), out_dtype),
        compiler_params=_params("parallel", "parallel"),
        name=name,
    )(*args)


def _ffn_gate(a, wg, wu, conv_w, conv_b, n_valid, *, tm, tn):
    m, k = a.shape
    n = wg.shape[1]
    hb = tm // FFN_HALO
    last = m // FFN_HALO - 1
    return pl.pallas_call(
        functools.partial(_ffn_gate_kernel, n_valid=n_valid),
        grid=(m // tm, n // tn),
        in_specs=[pl.BlockSpec((tm, k), lambda i, j: (i, 0), pipeline_mode=pl.Buffered(1)),
                  pl.BlockSpec((FFN_HALO, k), lambda i, j: (jnp.maximum(i * hb - 1, 0), 0)),
                  pl.BlockSpec((FFN_HALO, k), lambda i, j: (jnp.minimum((i + 1) * hb, last), 0)),
                  pl.BlockSpec((k, tn), lambda i, j: (0, j)),
                  pl.BlockSpec((k, tn), lambda i, j: (0, j)),
                  pl.BlockSpec((3, tn), lambda i, j: (0, j)),
                  pl.BlockSpec((1, tn), lambda i, j: (0, j))],
        out_specs=pl.BlockSpec((tm, tn), lambda i, j: (i, j)),
        out_shape=jax.ShapeDtypeStruct((m, n), bf16),
        scratch_shapes=[pltpu.VMEM((tm + 2 * FFN_HALO, k), bf16)],
        compiler_params=_params("parallel", "arbitrary"),
        name="ffn_gate_conv_act",
    )(a, a, a, wg, wu, conv_w, conv_b.reshape(1, n))


def _merge(ya, zb, wa, wb, proj, ga_off, gb_off, *, tm, tn):
    m, ka = ya.shape
    kb = zb.shape[1]
    n = wa.shape[1]
    ga_blk, gb_blk = ga_off // tn, gb_off // tn
    return pl.pallas_call(
        _merge_kernel,
        grid=(n // tn, m // tm),
        in_specs=[pl.BlockSpec((tm, ka), lambda j, i: (i, 0)),
                  pl.BlockSpec((tm, kb), lambda j, i: (i, 0)),
                  pl.BlockSpec((ka, tn), lambda j, i: (0, j)),
                  pl.BlockSpec((kb, tn), lambda j, i: (0, j)),
                  pl.BlockSpec((tm, tn), lambda j, i: (i, ga_blk + j)),
                  pl.BlockSpec((tm, tn), lambda j, i: (i, gb_blk + j))],
        out_specs=pl.BlockSpec((tm, tn), lambda j, i: (i, j)),
        out_shape=jax.ShapeDtypeStruct((m, n), bf16),
        compiler_params=_params("parallel", "parallel"),
        name="gated_merge",
    )(ya, zb, wa, wb, proj, proj)


def _rope_tables(n_tok, lp):
    rows = n_tok // GRID_W
    row = jnp.repeat(jnp.arange(rows, dtype=f32), GRID_W)
    col = jnp.tile(jnp.arange(GRID_W, dtype=f32), rows)
    lead = jnp.zeros((N_META_TOKENS,), f32)
    tail = jnp.zeros((lp - N_META_TOKENS - n_tok,), f32)
    row = jnp.concatenate([lead, row, tail])
    col = jnp.concatenate([lead, col, tail])
    inv_freq = ROPE_THETA ** (-jnp.arange(ROPE_HALF, dtype=f32) * 2.0 / (2 * ROPE_HALF))
    ar, ac = row[:, None] * inv_freq, col[:, None] * inv_freq
    cr, sr, cc, sc = jnp.cos(ar), jnp.sin(ar), jnp.cos(ac), jnp.sin(ac)
    zero = jnp.zeros_like(sr)
    cos = jnp.concatenate([cr, cr, cc, cc], axis=-1)
    sin_up = jnp.concatenate([-sr, zero, -sc, zero], axis=-1)
    sin_dn = jnp.concatenate([zero, sr, zero, sc], axis=-1)
    return cos, sin_up, sin_dn


def _qk_prep_kernel(x_ref, g_ref, cos_ref, sup_ref, sdn_ref, o_ref):
    cos, sup, sdn = cos_ref[...], sup_ref[...], sdn_ref[...]
    for h in range(N_Q_HEADS + N_KV_HEADS):
        cols = slice(h * HEAD_DIM, (h + 1) * HEAD_DIM)
        x = x_ref[:, cols]
        y = x * lax.rsqrt(jnp.mean(x * x, axis=-1, keepdims=True) + NORM_EPS) * g_ref[h]
        out = (y * cos + pltpu.roll(y, HEAD_DIM - ROPE_HALF, 1) * sup
               + pltpu.roll(y, ROPE_HALF, 1) * sdn)
        o_ref[:, cols] = out.astype(o_ref.dtype)


def _qk_prep(proj, gains, cos, sin_up, sin_dn):
    lp = proj.shape[0]
    width = (N_Q_HEADS + N_KV_HEADS) * HEAD_DIM
    tr = _pick(lp, (256, 128))
    tab = pl.BlockSpec((tr, HEAD_DIM), lambda i: (i, 0))
    return pl.pallas_call(
        _qk_prep_kernel,
        grid=(lp // tr,),
        in_specs=[pl.BlockSpec((tr, width), lambda i: (i, 0)),
                  pl.BlockSpec(gains.shape, lambda i: (0, 0, 0)),
                  tab, tab, tab],
        out_specs=pl.BlockSpec((tr, width), lambda i: (i, 0)),
        out_shape=jax.ShapeDtypeStruct((lp, width), bf16),
        compiler_params=_params("parallel"),
        name="qk_norm_rope",
    )(proj, gains, cos, sin_up, sin_dn)


SOFTMAX_ROWS = 16
FLASH_UNIT_ROWS = 256


def _flash_kernel(q_ref, k_ref, v_ref, bias_ref, o_ref, m_scr, alpha_scr, acc_scr, s_scr, p_scr):
    j = pl.program_id(2)
    last = pl.num_programs(2) - 1
    tq = s_scr.shape[1]

    @pl.when(j == 0)
    def _():
        m_scr[...] = jnp.full(m_scr.shape, -jnp.inf, f32)
        acc_scr[...] = jnp.zeros(acc_scr.shape, f32)

    unit_rows = min(tq, FLASH_UNIT_ROWS)
    units = [(g, slice(r, r + unit_rows)) for g in range(GQA_GROUP) for r in range(0, tq, unit_rows)]

    def scores(u, masked):
        g, rows = u
        q = q_ref[rows, g * HEAD_DIM:(g + 1) * HEAD_DIM]
        s = lax.dot_general(q, k_ref[...], (((1,), (1,)), ((), ())), preferred_element_type=f32)
        if masked:
            s = s + bias_ref[...]
        s_scr[g, rows, :] = s
        m_prev = m_scr[g, rows, :]
        m_new = jnp.maximum(m_prev, jnp.max(s, axis=-1, keepdims=True))
        alpha_scr[g, rows, :] = jnp.exp2(m_prev - m_new)
        m_scr[g, rows, :] = m_new

    def softmax(u):
        g, unit = u
        for r in range(unit.start, unit.stop, SOFTMAX_ROWS):
            rows = slice(r, r + SOFTMAX_ROWS)
            p_scr[g, rows, :] = jnp.exp2(s_scr[g, rows, :] - m_scr[g, rows, :]).astype(bf16)

    def weighted_sum(u):
        g, rows = u
        acc_scr[g, rows, :] = (alpha_scr[g, rows, :] * acc_scr[g, rows, :]
                               + _dot(p_scr[g, rows, :], v_ref[...]))

    def step(masked):
        scores(units[0], masked)
        for i, u in enumerate(units):
            if i + 1 < len(units):
                scores(units[i + 1], masked)
            softmax(u)
            weighted_sum(u)

    pl.when(j < last)(lambda: step(False))
    pl.when(j == last)(lambda: step(True))

    @pl.when(j == last)
    def _():
        for g in range(GQA_GROUP):
            acc = acc_scr[g]
            o_ref[:, g * HEAD_DIM:(g + 1) * HEAD_DIM] = (
                acc[:, :HEAD_DIM] / acc[:, HEAD_DIM:HEAD_DIM + 1]).astype(o_ref.dtype)


def _flash_attention(qk, v_ext, key_bias, *, tq, tk):
    lp = qk.shape[0]
    gw = GQA_GROUP * HEAD_DIM
    vw = 2 * HEAD_DIM
    return pl.pallas_call(
        _flash_kernel,
        grid=(N_KV_HEADS, lp // tq, lp // tk),
        in_specs=[pl.BlockSpec((tq, gw), lambda h, i, j: (i, h)),
                  pl.BlockSpec((tk, HEAD_DIM), lambda h, i, j: (j, N_Q_HEADS + h)),
                  pl.BlockSpec((tk, vw), lambda h, i, j: (j, h)),
                  pl.BlockSpec((1, tk), lambda h, i, j: (0, j))],
        out_specs=pl.BlockSpec((tq, gw), lambda h, i, j: (i, h)),
        out_shape=jax.ShapeDtypeStruct((lp, ATTN_WIDTH), bf16),
        scratch_shapes=[pltpu.VMEM((GQA_GROUP, tq, 1), f32),
                        pltpu.VMEM((GQA_GROUP, tq, 1), f32),
                        pltpu.VMEM((GQA_GROUP, tq, vw), f32),
                        pltpu.VMEM((GQA_GROUP, tq, tk), f32),
                        pltpu.VMEM((GQA_GROUP, tq, tk), bf16)],
        compiler_params=_params("parallel", "parallel", "arbitrary"),
        name="flash_gqa",
    )(qk, qk, v_ext, key_bias)


def _conv3_rows(x, prev, nxt, w, row0, n_valid):
    tr = x.shape[0]
    loc = lax.broadcasted_iota(jnp.int32, (tr, 1), 0)
    t = loc + row0
    xm = jnp.where(loc == 0, prev[SUBLANES - 1:SUBLANES, :], pltpu.roll(x, 1, 0))
    xm = jnp.where(t == 0, 0.0, xm)
    xp = jnp.where(loc == tr - 1, nxt[0:1, :], pltpu.roll(x, tr - 1, 0))
    xp = jnp.where(t >= n_valid - 1, 0.0, xp)
    return w[0:1, :] * xm + w[1:2, :] * x + w[2:3, :] * xp


def _halo_specs(tr, tc, n_rows, col_blk0):
    rb = tr // SUBLANES
    last = n_rows // SUBLANES - 1
    return [pl.BlockSpec((tr, tc), lambda i, j: (jnp.minimum(i, n_rows // tr - 1), col_blk0 + j)),
            pl.BlockSpec((SUBLANES, tc), lambda i, j: (jnp.clip(i * rb - 1, 0, last), col_blk0 + j)),
            pl.BlockSpec((SUBLANES, tc), lambda i, j: (jnp.minimum((i + 1) * rb, last), col_blk0 + j))]


def _hyena_conv_kernel(x_ref, p_ref, n_ref, w_ref, b_ref, o_ref, *, n_valid):
    tr = x_ref.shape[0]
    row0 = pl.program_id(0) * tr
    y = _conv3_rows(x_ref[...], p_ref[...], n_ref[...], w_ref[...], row0, n_valid) + b_ref[...]
    t = row0 + lax.broadcasted_iota(jnp.int32, (tr, 1), 0)
    o_ref[...] = jnp.where(t < n_valid, y, 0.0)


def _hyena_conv(proj, w, b, col_off, n_valid, out_rows):
    lp = proj.shape[0]
    width = w.shape[1]
    tc = _pick(math.gcd(col_off, width), (1024, 512, 256, 128))
    tr = _pick(math.gcd(lp, out_rows), (512, 256, 128))
    return pl.pallas_call(
        functools.partial(_hyena_conv_kernel, n_valid=n_valid),
        grid=(out_rows // tr, width // tc),
        in_specs=_halo_specs(tr, tc, lp, col_off // tc) + [
            pl.BlockSpec((3, tc), lambda i, j: (0, j)),
            pl.BlockSpec((1, tc), lambda i, j: (0, j))],
        out_specs=pl.BlockSpec((tr, tc), lambda i, j: (i, j)),
        out_shape=jax.ShapeDtypeStruct((out_rows, width), f32),
        compiler_params=_params("parallel", "parallel"),
        name="hyena_short_conv",
    )(proj, proj, proj, w, b.reshape(1, width))


EMB_COLS = LANES
COL_FWD = FILTER_EMB_DIM
COL_BWD = FILTER_EMB_DIM + 1
COL_T = FILTER_EMB_DIM + 2


def _filter_inputs(seq_len, n_fft, n_rows):
    n = jnp.arange(n_rows)
    fwd = n < seq_len
    bwd = (n > n_fft - seq_len) & (n < n_fft)
    lag = jnp.where(fwd, n, jnp.where(bwd, n_fft - n, 0)).astype(f32)[:, None]
    t = lag / (seq_len - 1)
    w = 2.0 * math.pi * lag / seq_len
    f = jnp.linspace(1e-4, FILTER_BANDS - 1, FILTER_BANDS, dtype=f32)[None, :]
    cols = [t, jnp.cos(f * w), -jnp.sin(f * w), fwd.astype(f32)[:, None], bwd.astype(f32)[:, None], t,
            jnp.zeros((n_rows, EMB_COLS - FILTER_EMB_DIM - 3), f32)]
    return jnp.concatenate(cols, axis=-1)


def _filter_kernel(e_ref, w1_ref, b1_ref, w2_ref, b2_ref, w3_ref, b3_ref, w4_ref, fr_ref, ad_ref,
                   o0_ref, o1_ref):
    e = e_ref[...]
    freq = fr_ref[...]

    def layer(x, w_ref, b_ref):
        xh, xl = _split(x)
        wh, wl = _split(w_ref[...])
        return jnp.sin(freq * (_dot3(xh, xl, wh, wl) + b_ref[...]))

    hid = layer(layer(layer(e, w1_ref, b1_ref), w2_ref, b2_ref), w3_ref, b3_ref)
    hid = hid.astype(bf16)
    scale = (e[:, COL_FWD:COL_FWD + 1] + e[:, COL_BWD:COL_BWD + 1]) * jnp.exp(
        -e[:, COL_T:COL_T + 1] * ad_ref[...])
    o0_ref[...] = _dot(hid, w4_ref[0]) * scale
    o1_ref[...] = _dot(hid, w4_ref[1]) * scale


def _filters(e_ext, w1, b1, w2, b2, w3, b3, w4, freq, n_ch, seq_len):
    n_rows = e_ext.shape[0]
    hid = w2.shape[0]
    tr = LANES
    w1p = jnp.zeros((EMB_COLS, hid), f32).at[:FILTER_EMB_DIM].set(w1)
    w4r = w4.reshape(hid, 2, 2, n_ch).transpose(1, 2, 0, 3).astype(bf16)
    abs_delta = jnp.abs(jnp.linspace(MIN_DECAY, MAX_DECAY, n_ch, dtype=f32)).reshape(1, n_ch)
    full = lambda *shape: pl.BlockSpec(shape, lambda i: (0,) * len(shape))
    fwd_blocks = -(-seq_len // tr)
    w4_spec = pl.BlockSpec((2, None, hid, n_ch), lambda i: (0, jnp.where(i < fwd_blocks, 0, 1), 0, 0))
    out_spec = pl.BlockSpec((tr, n_ch), lambda i: (i, 0))
    return pl.pallas_call(
        _filter_kernel,
        grid=(n_rows // tr,),
        in_specs=[pl.BlockSpec((tr, EMB_COLS), lambda i: (i, 0)),
                  full(EMB_COLS, hid), full(1, hid), full(hid, hid), full(1, hid),
                  full(hid, hid), full(1, hid), w4_spec, full(1, hid), full(1, n_ch)],
        out_specs=[out_spec, out_spec],
        out_shape=[jax.ShapeDtypeStruct((n_rows, n_ch), f32)] * 2,
        compiler_params=_params("parallel"),
        name="hyena_filters",
    )(e_ext, w1p, b1.reshape(1, hid), w2, b2.reshape(1, hid), w3, b3.reshape(1, hid), w4r,
      freq.reshape(1, hid), abs_delta)


def _dft_tables(n1, nz):
    n_fft = n1 * LANES
    k1n = (n1 + 1) // 2
    two_pi = 2.0 * math.pi

    def angle(m, period):
        m = jnp.where(m > period // 2, m - period, m)
        return m.astype(f32) * (two_pi / period)

    row = jnp.arange(2 * k1n * SUBLANES, dtype=jnp.int32)[:, None]
    col = jnp.arange(n1 * SUBLANES, dtype=jnp.int32)[None, :]
    k1r, part = row // (2 * SUBLANES), (row // SUBLANES) % 2
    a1 = angle((k1r * (col // SUBLANES)) % n1, n1)
    fwd = jnp.where(row % SUBLANES == col % SUBLANES,
                    jnp.where(part == 0, jnp.cos(a1), -jnp.sin(a1)), 0.0)
    wgt = jnp.where(k1r == 0, 1.0, 2.0).astype(f32) / n_fft
    inv = (fwd * wgt)[:, :nz * SUBLANES].T
    fwd, inv = fwd.astype(bf16), inv.astype(bf16)

    k1 = jnp.arange(k1n, dtype=jnp.int32)
    n2 = jnp.arange(LANES, dtype=jnp.int32)
    k2 = jnp.arange(LANES, dtype=jnp.int32)
    m = (n2[None, None, :] * (k2[None, :, None] * n1 + k1[:, None, None])) % n_fft
    a2 = angle(m, n_fft)
    gr, gi = jnp.cos(a2), -jnp.sin(a2)
    g = jnp.concatenate([jnp.concatenate([gr, -gi], axis=2),
                         jnp.concatenate([gi, gr], axis=2)], axis=1)
    grt, git = jnp.swapaxes(gr, 1, 2), jnp.swapaxes(gi, 1, 2)
    ginv = jnp.concatenate([jnp.concatenate([grt, git], axis=2),
                            jnp.concatenate([-git, grt], axis=2)], axis=1)
    return fwd, g.astype(bf16), ginv.astype(bf16), inv


def _resident(shape):
    return pl.BlockSpec(shape, lambda *_: (0,) * len(shape), pipeline_mode=pl.Buffered(1))


def _dft_rows_kernel(m_ref, x_ref, o_ref):
    kd, sub, tw = x_ref.shape
    x = x_ref[...].reshape(kd * sub, tw).astype(bf16)
    o_ref[...] = _dot(m_ref[...], x).reshape(o_ref.shape)


def _dft_rows(mat8, x3, n_ch, *, slot, tw):
    r, kd = mat8.shape[0] // SUBLANES, mat8.shape[1] // SUBLANES
    ct = n_ch // tw
    return pl.pallas_call(
        _dft_rows_kernel,
        grid=(LANES // SUBLANES, ct),
        in_specs=[_resident(mat8.shape),
                  pl.BlockSpec((kd, SUBLANES, tw), lambda jb, c: (0, jb, slot * ct + c))],
        out_specs=pl.BlockSpec((r, SUBLANES, tw), lambda jb, c: (0, jb, c)),
        out_shape=jax.ShapeDtypeStruct((r, LANES, n_ch), f32),
        compiler_params=_params("parallel", "parallel"),
        name="dft_outer_fwd",
    )(mat8, x3)


def _spectrum_kernel(g_ref, a_ref, o_ref):
    o_ref[...] = _dot(g_ref[...], a_ref[...].astype(bf16))


def _spectrum(g, a3, *, tc):
    k1n, two_l, n_ch = a3.shape
    gspec = pl.BlockSpec((None, two_l, two_l), lambda k, c: (k, 0, 0))
    dspec = pl.BlockSpec((None, two_l, tc), lambda k, c: (k, 0, c))
    return pl.pallas_call(
        _spectrum_kernel,
        grid=(k1n, n_ch // tc),
        in_specs=[gspec, dspec],
        out_specs=dspec,
        out_shape=jax.ShapeDtypeStruct(a3.shape, f32),
        compiler_params=_params("parallel", "parallel"),
        name="dft_inner_fwd",
    )(g, a3)


def _spectral_conv_kernel(g_ref, ginv_ref, a_ref, kf_ref, o_ref):
    x = _dot(g_ref[...], a_ref[...].astype(bf16))
    kf = kf_ref[...]
    xr, xi = x[:LANES], x[LANES:]
    kr, ki = kf[:LANES], kf[LANES:]
    y = jnp.concatenate([xr * kr - xi * ki, xr * ki + xi * kr], axis=0)
    o_ref[...] = _dot(ginv_ref[...], y.astype(bf16))


def _spectral_conv(g, ginv, a3, kf3, *, tc):
    k1n, two_l, n_ch = a3.shape
    gspec = pl.BlockSpec((None, two_l, two_l), lambda k, c: (k, 0, 0))
    dspec = pl.BlockSpec((None, two_l, tc), lambda k, c: (k, 0, c))
    return pl.pallas_call(
        _spectral_conv_kernel,
        grid=(k1n, n_ch // tc),
        in_specs=[gspec, gspec, dspec, dspec],
        out_specs=dspec,
        out_shape=jax.ShapeDtypeStruct(a3.shape, f32),
        compiler_params=_params("parallel", "parallel"),
        name="dft_inner_conv",
    )(g, ginv, a3, kf3)


def _idft_gate_kernel(m_ref, b_ref, z_ref, gate_ref, skip_ref, o_ref):
    r, sub, tw = b_ref.shape
    y = _dot(m_ref[...], b_ref[...].reshape(r * sub, tw).astype(bf16)).reshape(o_ref.shape)
    o_ref[...] = (gate_ref[...] * (y + skip_ref[...] * z_ref[...])).astype(o_ref.dtype)


def _idft_gate(mat8, b3, z3, hyc3, gate_slot, skip, out_dtype, *, tw):
    nz, r = mat8.shape[0] // SUBLANES, mat8.shape[1] // SUBLANES
    n_ch = skip.shape[1]
    ct = n_ch // tw
    tile = lambda c_of: pl.BlockSpec((nz, SUBLANES, tw), lambda jb, c: (0, jb, c_of(c)))
    return pl.pallas_call(
        _idft_gate_kernel,
        grid=(LANES // SUBLANES, ct),
        in_specs=[_resident(mat8.shape),
                  pl.BlockSpec((r, SUBLANES, tw), lambda jb, c: (0, jb, c)),
                  tile(lambda c: c),
                  tile(lambda c: gate_slot * ct + c),
                  pl.BlockSpec((1, tw), lambda jb, c: (0, c))],
        out_specs=tile(lambda c: c),
        out_shape=jax.ShapeDtypeStruct((nz, LANES, n_ch), out_dtype),
        compiler_params=_params("parallel", "parallel"),
        name="dft_outer_inv_gate",
    )(mat8, b3, z3, hyc3, skip)


def kernel(x, meta_tokens, norm_mix, w_in, q_norm, k_norm, hyena_conv_w, hyena_conv_b, filt_w1, filt_b1, filt_w2, filt_b2, filt_w3, filt_b3, filt_w4, filt_freq, hyena_skip, w_attn_branch, w_hyena_branch, w_out, norm_ffn, w_ffn_gate, w_ffn_up, ffn_conv_w, ffn_conv_b, w_ffn_down, norm_final):
    batch, n_tok, d_model = x.shape
    depth = norm_mix.shape[0]
    n_ch = w_hyena_branch.shape[1]
    d_ff = w_ffn_gate.shape[2]
    seq_len = n_tok + N_META_TOKENS
    lp = _round_up(seq_len, ROW_PAD)
    nz = lp // LANES
    n1 = -(-(seq_len - 1 + _round_up(seq_len, LANES)) // LANES) | 1
    n_fft = n1 * LANES
    k1n = (n1 + 1) // 2
    hy_off = ATTN_WIDTH + 2 * KV_WIDTH
    ga_off = hy_off + 3 * n_ch
    gb_off = ga_off + d_model

    cos, sin_up, sin_dn = _rope_tables(n_tok, lp)
    key_bias = jnp.where(jnp.arange(lp) < seq_len, 0.0, MASK_BIAS).astype(f32).reshape(1, lp)
    e_ext = _filter_inputs(seq_len, n_fft, n_fft)
    fwd_m, g_m, ginv_m, inv_m = _dft_tables(n1, nz)
    fwd_z = fwd_m[:, :nz * SUBLANES]

    tm = _pick(lp, (512, 256, 128))
    tq = _pick(lp, (512, 256, 128))
    tk = _pick(lp, (1536, 1024, 512, 256, 128))
    assert lp - tk < seq_len, "every key tile must hold at least one real token"
    tw = _pick(n_ch, (2048, 1024, 512, 256, 128))
    tw_outer = _pick(n_ch, (512, 256, 128))

    outs = []
    for b in range(batch):
        h = jnp.concatenate([meta_tokens.astype(f32), x[b],
                             jnp.zeros((lp - seq_len, d_model), f32)], axis=0)
        for i in range(depth):
            u = _rmsnorm(h, norm_mix[i], seq_len, bf16)
            in_w = w_in[i].shape[1]
            proj = _matmul(u, w_in[i].astype(bf16), f32, tm=tm,
                           tn=_pick(in_w, (1024, 512, 256, 128)), name="in_proj")

            q_gain = q_norm[i] * (HEAD_DIM ** -0.5 * math.log2(math.e))
            gains = jnp.concatenate([jnp.broadcast_to(q_gain, (N_Q_HEADS, HEAD_DIM)),
                                     jnp.broadcast_to(k_norm[i], (N_KV_HEADS, HEAD_DIM))])
            qk = _qk_prep(proj, gains.reshape(-1, 1, HEAD_DIM), cos, sin_up, sin_dn)
            v = proj[:, ATTN_WIDTH + KV_WIDTH:hy_off].astype(bf16).reshape(lp, N_KV_HEADS, HEAD_DIM)
            v_ext = jnp.concatenate([v, jnp.ones((lp, N_KV_HEADS, 1), bf16),
                                     jnp.zeros((lp, N_KV_HEADS, HEAD_DIM - 1), bf16)], axis=-1)
            y_attn = _flash_attention(qk, v_ext.reshape(lp, N_KV_HEADS * 2 * HEAD_DIM), key_bias,
                                      tq=tq, tk=tk)

            hyc = _hyena_conv(proj, hyena_conv_w[i], hyena_conv_b[i], hy_off, seq_len, lp)
            hyc3 = hyc.reshape(nz, LANES, 3 * n_ch)
            filt = _filters(e_ext, filt_w1[i], filt_b1[i], filt_w2[i], filt_b2[i], filt_w3[i],
                            filt_b3[i], filt_w4[i], filt_freq[i], n_ch, seq_len)
            z3 = hyc3
            for o in range(2):
                kf_a = _dft_rows(fwd_m, filt[o].reshape(n1, LANES, n_ch), n_ch, slot=0, tw=tw_outer)
                kf3 = _spectrum(g_m, kf_a.reshape(k1n, 2 * LANES, n_ch), tc=tw)
                za = _dft_rows(fwd_z, z3, n_ch, slot=0, tw=tw_outer)
                zb = _spectral_conv(g_m, ginv_m, za.reshape(k1n, 2 * LANES, n_ch), kf3, tc=tw)
                z3 = _idft_gate(inv_m, zb.reshape(2 * k1n, LANES, n_ch), z3, hyc3, 1 + o,
                                hyena_skip[i, o].reshape(1, n_ch), f32, tw=tw_outer)
            z = z3.reshape(lp, n_ch)

            tn_m = _pick(math.gcd(math.gcd(ga_off, gb_off), d_model), (1024, 512, 256, 128))
            mixed = _merge(y_attn, z, w_attn_branch[i].astype(bf16), w_hyena_branch[i].astype(bf16),
                           proj, ga_off, gb_off, tm=tm, tn=tn_m)
            h = _matmul(mixed, w_out[i].astype(bf16), f32, tm=tm,
                        tn=_pick(d_model, (1024, 512, 256, 128)), res=h, name="out_proj")

            u2 = _rmsnorm(h, norm_ffn[i], seq_len, bf16)
            act = _ffn_gate(u2, w_ffn_gate[i].astype(bf16), w_ffn_up[i].astype(bf16),
                            ffn_conv_w[i], ffn_conv_b[i], seq_len,
                            tm=_pick(lp, (1536, 1024, 512, 256, 128)), tn=_pick(d_ff, (256, 128)))
            h = _matmul(act, w_ffn_down[i].astype(bf16), f32, tm=tm,
                        tn=_pick(d_model, (512, 256, 128)), res=h, name="ffn_down")
        outs.append(_final_norm(h, norm_final, n_tok)[None])
    return outs[0] if batch == 1 else jnp.concatenate(outs, axis=0)
```

```python
import functools
import math

import jax
import jax.numpy as jnp
from jax import lax
from jax.experimental import pallas as pl
from jax.experimental.pallas import tpu as pltpu

f32 = jnp.float32
bf16 = jnp.bfloat16

N_META_TOKENS = 16
GRID_W = 64
HEAD_DIM = 128
N_Q_HEADS = 16
N_KV_HEADS = 4
GQA_GROUP = N_Q_HEADS // N_KV_HEADS
ATTN_WIDTH = N_Q_HEADS * HEAD_DIM
KV_WIDTH = N_KV_HEADS * HEAD_DIM
ROPE_THETA = 10000.0
ROPE_HALF = HEAD_DIM // 4
FILTER_EMB_DIM = 33
FILTER_BANDS = (FILTER_EMB_DIM - 1) // 2
DECAY_TARGET = 1e-2
MIN_DECAY = math.log(DECAY_TARGET) / 1.5
MAX_DECAY = math.log(DECAY_TARGET) / 0.3
NORM_EPS = 1e-6

LANES = 128
SUBLANES = 8
V7X_VMEM_LIMIT_BYTES = 56 * 1024 * 1024
ROW_PAD = 512
MASK_BIAS = -1e30


def _round_up(x, m):
    return (x + m - 1) // m * m


def _pick(n, cands):
    for c in cands:
        if n % c == 0:
            return c
    return n


def _params(*sem):
    return pltpu.CompilerParams(dimension_semantics=sem, vmem_limit_bytes=V7X_VMEM_LIMIT_BYTES)


def _split(x):
    hi = x.astype(bf16)
    lo = (x - hi.astype(f32)).astype(bf16)
    return hi, lo


def _dot(a, b):
    return jnp.dot(a, b, preferred_element_type=f32)


def _dot3(ah, al, bh, bl):
    return _dot(ah, bh) + _dot(ah, bl) + _dot(al, bh)


def _rmsnorm_kernel(x_ref, g_ref, o_ref, *, n_valid):
    tr = x_ref.shape[0]
    x = x_ref[...]
    y = x * lax.rsqrt(jnp.mean(x * x, axis=-1, keepdims=True) + NORM_EPS) * g_ref[...]
    rows = pl.program_id(0) * tr + lax.broadcasted_iota(jnp.int32, (tr, 1), 0)
    o_ref[...] = jnp.where(rows < n_valid, y, 0.0).astype(o_ref.dtype)


def _rmsnorm(x, g, n_valid, out_dtype):
    m, d = x.shape
    tr = _pick(m, (256, 128, 64, 32, 16))
    return pl.pallas_call(
        functools.partial(_rmsnorm_kernel, n_valid=n_valid),
        grid=(m // tr,),
        in_specs=[pl.BlockSpec((tr, d), lambda i: (i, 0)),
                  pl.BlockSpec((1, d), lambda i: (0, 0))],
        out_specs=pl.BlockSpec((tr, d), lambda i: (i, 0)),
        out_shape=jax.ShapeDtypeStruct((m, d), out_dtype),
        compiler_params=_params("parallel"),
        name="rmsnorm",
    )(x, g.reshape(1, d))


def _final_norm_kernel(x_ref, nxt_ref, g_ref, o_ref):
    tr = x_ref.shape[0]

    def norm(x):
        return x * lax.rsqrt(jnp.mean(x * x, axis=-1, keepdims=True) + NORM_EPS) * g_ref[...]

    o_ref[0:tr - N_META_TOKENS, :] = norm(x_ref[N_META_TOKENS:tr, :])
    o_ref[tr - N_META_TOKENS:tr, :] = norm(nxt_ref[...])


def _final_norm(h, g, n_tok):
    d = h.shape[1]
    tr = _pick(n_tok, (256, 128))
    assert n_tok % tr == 0 and tr % N_META_TOKENS == 0 and n_tok + N_META_TOKENS <= h.shape[0]
    nb = tr // N_META_TOKENS
    return pl.pallas_call(
        _final_norm_kernel,
        grid=(n_tok // tr,),
        in_specs=[pl.BlockSpec((tr, d), lambda i: (i, 0)),
                  pl.BlockSpec((N_META_TOKENS, d), lambda i: ((i + 1) * nb, 0)),
                  pl.BlockSpec((1, d), lambda i: (0, 0))],
        out_specs=pl.BlockSpec((tr, d), lambda i: (i, 0)),
        out_shape=jax.ShapeDtypeStruct((n_tok, d), h.dtype),
        compiler_params=_params("parallel"),
        name="final_rmsnorm",
    )(h, h, g.reshape(1, d))


def _mm_kernel(a_ref, b_ref, o_ref):
    o_ref[...] = _dot(a_ref[...], b_ref[...]).astype(o_ref.dtype)


def _mm_res_kernel(a_ref, b_ref, r_ref, o_ref):
    o_ref[...] = r_ref[...] + _dot(a_ref[...], b_ref[...])


FFN_HALO = 16


def _ffn_gate_kernel(a_ref, ap_ref, an_ref, wg_ref, wu_ref, cw_ref, cb_ref, o_ref, lhs_scr, *, n_valid):
    i, j = pl.program_id(0), pl.program_id(1)
    tm = a_ref.shape[0]
    ext = tm + 2 * FFN_HALO

    @pl.when(j == 0)
    def _():
        lhs_scr[0:FFN_HALO, :] = ap_ref[...]
        lhs_scr[FFN_HALO:FFN_HALO + tm, :] = a_ref[...]
        lhs_scr[FFN_HALO + tm:ext, :] = an_ref[...]

    g_ext = _dot(lhs_scr[...], wg_ref[...])
    up = _dot(lhs_scr[FFN_HALO:FFN_HALO + tm, :], wu_ref[...])
    core = slice(FFN_HALO, FFN_HALO + tm)
    t = i * tm + lax.broadcasted_iota(jnp.int32, (tm, 1), 0)
    g_prev = jnp.where(t == 0, 0.0, pltpu.roll(g_ext, 1, 0)[core])
    g_next = jnp.where(t >= n_valid - 1, 0.0, pltpu.roll(g_ext, ext - 1, 0)[core])
    cw = cw_ref[...]
    gate = cw[0:1, :] * g_prev + cw[1:2, :] * g_ext[core] + cw[2:3, :] * g_next + cb_ref[...]
    o_ref[...] = (gate * jax.nn.sigmoid(gate) * up).astype(o_ref.dtype)


def _merge_kernel(ya_ref, zb_ref, wa_ref, wb_ref, ga_ref, gb_ref, o_ref):
    a = _dot(ya_ref[...], wa_ref[...])
    b = _dot(zb_ref[...].astype(bf16), wb_ref[...])
    o_ref[...] = (jax.nn.sigmoid(ga_ref[...]) * a + jax.nn.sigmoid(gb_ref[...]) * b).astype(o_ref.dtype)


def _matmul(a, b, out_dtype, *, tm, tn, res=None, name):
    m, k = a.shape
    n = b.shape[1]
    in_specs = [pl.BlockSpec((tm, k), lambda j, i: (i, 0)),
                pl.BlockSpec((k, tn), lambda j, i: (0, j))]
    args = [a, b]
    kern = _mm_kernel
    if res is not None:
        in_specs.append(pl.BlockSpec((tm, tn), lambda j, i: (i, j)))
        args.append(res)
        kern = _mm_res_kernel
    return pl.pallas_call(
        kern,
        grid=(n // tn, m // tm),
        in_specs=in_specs,
        out_specs=pl.BlockSpec((tm, tn), lambda j, i: (i, j)),
        out_shape=jax.ShapeDtypeStruct((m, n), out_dtype),
        compiler_params=_params("parallel", "parallel"),
        name=name,
    )(*args)


def _ffn_gate(a, wg, wu, conv_w, conv_b, n_valid, *, tm, tn):
    m, k = a.shape
    n = wg.shape[1]
    hb = tm // FFN_HALO
    last = m // FFN_HALO - 1
    return pl.pallas_call(
        functools.partial(_ffn_gate_kernel, n_valid=n_valid),
        grid=(m // tm, n // tn),
        in_specs=[pl.BlockSpec((tm, k), lambda i, j: (i, 0), pipeline_mode=pl.Buffered(1)),
                  pl.BlockSpec((FFN_HALO, k), lambda i, j: (jnp.maximum(i * hb - 1, 0), 0)),
                  pl.BlockSpec((FFN_HALO, k), lambda i, j: (jnp.minimum((i + 1) * hb, last), 0)),
                  pl.BlockSpec((k, tn), lambda i, j: (0, j)),
                  pl.BlockSpec((k, tn), lambda i, j: (0, j)),
                  pl.BlockSpec((3, tn), lambda i, j: (0, j)),
                  pl.BlockSpec((1, tn), lambda i, j: (0, j))],
        out_specs=pl.BlockSpec((tm, tn), lambda i, j: (i, j)),
        out_shape=jax.ShapeDtypeStruct((m, n), bf16),
        scratch_shapes=[pltpu.VMEM((tm + 2 * FFN_HALO, k), bf16)],
        compiler_params=_params("parallel", "arbitrary"),
        name="ffn_gate_conv_act",
    )(a, a, a, wg, wu, conv_w, conv_b.reshape(1, n))


def _merge(ya, zb, wa, wb, proj, ga_off, gb_off, *, tm, tn):
    m, ka = ya.shape
    kb = zb.shape[1]
    n = wa.shape[1]
    ga_blk, gb_blk = ga_off // tn, gb_off // tn
    return pl.pallas_call(
        _merge_kernel,
        grid=(n // tn, m // tm),
        in_specs=[pl.BlockSpec((tm, ka), lambda j, i: (i, 0)),
                  pl.BlockSpec((tm, kb), lambda j, i: (i, 0)),
                  pl.BlockSpec((ka, tn), lambda j, i: (0, j)),
                  pl.BlockSpec((kb, tn), lambda j, i: (0, j)),
                  pl.BlockSpec((tm, tn), lambda j, i: (i, ga_blk + j)),
                  pl.BlockSpec((tm, tn), lambda j, i: (i, gb_blk + j))],
        out_specs=pl.BlockSpec((tm, tn), lambda j, i: (i, j)),
        out_shape=jax.ShapeDtypeStruct((m, n), bf16),
        compiler_params=_params("parallel", "parallel"),
        name="gated_merge",
    )(ya, zb, wa, wb, proj, proj)


def _rope_tables(n_tok, lp):
    rows = n_tok // GRID_W
    row = jnp.repeat(jnp.arange(rows, dtype=f32), GRID_W)
    col = jnp.tile(jnp.arange(GRID_W, dtype=f32), rows)
    lead = jnp.zeros((N_META_TOKENS,), f32)
    tail = jnp.zeros((lp - N_META_TOKENS - n_tok,), f32)
    row = jnp.concatenate([lead, row, tail])
    col = jnp.concatenate([lead, col, tail])
    inv_freq = ROPE_THETA ** (-jnp.arange(ROPE_HALF, dtype=f32) * 2.0 / (2 * ROPE_HALF))
    ar, ac = row[:, None] * inv_freq, col[:, None] * inv_freq
    cr, sr, cc, sc = jnp.cos(ar), jnp.sin(ar), jnp.cos(ac), jnp.sin(ac)
    zero = jnp.zeros_like(sr)
    cos = jnp.concatenate([cr, cr, cc, cc], axis=-1)
    sin_up = jnp.concatenate([-sr, zero, -sc, zero], axis=-1)
    sin_dn = jnp.concatenate([zero, sr, zero, sc], axis=-1)
    return cos, sin_up, sin_dn


def _qk_prep_kernel(x_ref, g_ref, cos_ref, sup_ref, sdn_ref, o_ref):
    cos, sup, sdn = cos_ref[...], sup_ref[...], sdn_ref[...]
    for h in range(N_Q_HEADS + N_KV_HEADS):
        cols = slice(h * HEAD_DIM, (h + 1) * HEAD_DIM)
        x = x_ref[:, cols]
        y = x * lax.rsqrt(jnp.mean(x * x, axis=-1, keepdims=True) + NORM_EPS) * g_ref[h]
        out = (y * cos + pltpu.roll(y, HEAD_DIM - ROPE_HALF, 1) * sup
               + pltpu.roll(y, ROPE_HALF, 1) * sdn)
        o_ref[:, cols] = out.astype(o_ref.dtype)


def _qk_prep(proj, gains, cos, sin_up, sin_dn):
    lp = proj.shape[0]
    width = (N_Q_HEADS + N_KV_HEADS) * HEAD_DIM
    tr = _pick(lp, (256, 128))
    tab = pl.BlockSpec((tr, HEAD_DIM), lambda i: (i, 0))
    return pl.pallas_call(
        _qk_prep_kernel,
        grid=(lp // tr,),
        in_specs=[pl.BlockSpec((tr, width), lambda i: (i, 0)),
                  pl.BlockSpec(gains.shape, lambda i: (0, 0, 0)),
                  tab, tab, tab],
        out_specs=pl.BlockSpec((tr, width), lambda i: (i, 0)),
        out_shape=jax.ShapeDtypeStruct((lp, width), bf16),
        compiler_params=_params("parallel"),
        name="qk_norm_rope",
    )(proj, gains, cos, sin_up, sin_dn)


SOFTMAX_ROWS = 16
FLASH_UNIT_ROWS = 256


def _flash_kernel(q_ref, k_ref, v_ref, bias_ref, o_ref, m_scr, alpha_scr, acc_scr, s_scr, p_scr):
    j = pl.program_id(2)
    last = pl.num_programs(2) - 1
    tq = s_scr.shape[1]

    @pl.when(j == 0)
    def _():
        m_scr[...] = jnp.full(m_scr.shape, -jnp.inf, f32)
        acc_scr[...] = jnp.zeros(acc_scr.shape, f32)

    unit_rows = min(tq, FLASH_UNIT_ROWS)
    units = [(g, slice(r, r + unit_rows)) for g in range(GQA_GROUP) for r in range(0, tq, unit_rows)]

    def scores(u, masked):
        g, rows = u
        q = q_ref[rows, g * HEAD_DIM:(g + 1) * HEAD_DIM]
        s = lax.dot_general(q, k_ref[...], (((1,), (1,)), ((), ())), preferred_element_type=f32)
        if masked:
            s = s + bias_ref[...]
        s_scr[g, rows, :] = s
        m_prev = m_scr[g, rows, :]
        m_new = jnp.maximum(m_prev, jnp.max(s, axis=-1, keepdims=True))
        alpha_scr[g, rows, :] = jnp.exp2(m_prev - m_new)
        m_scr[g, rows, :] = m_new

    def softmax(u):
        g, unit = u
        for r in range(unit.start, unit.stop, SOFTMAX_ROWS):
            rows = slice(r, r + SOFTMAX_ROWS)
            p_scr[g, rows, :] = jnp.exp2(s_scr[g, rows, :] - m_scr[g, rows, :]).astype(bf16)

    def weighted_sum(u):
        g, rows = u
        acc_scr[g, rows, :] = (alpha_scr[g, rows, :] * acc_scr[g, rows, :]
                               + _dot(p_scr[g, rows, :], v_ref[...]))

    def step(masked):
        scores(units[0], masked)
        for i, u in enumerate(units):
            if i + 1 < len(units):
                scores(units[i + 1], masked)
            softmax(u)
            weighted_sum(u)

    pl.when(j < last)(lambda: step(False))
    pl.when(j == last)(lambda: step(True))

    @pl.when(j == last)
    def _():
        for g in range(GQA_GROUP):
            acc = acc_scr[g]
            o_ref[:, g * HEAD_DIM:(g + 1) * HEAD_DIM] = (
                acc[:, :HEAD_DIM] / acc[:, HEAD_DIM:HEAD_DIM + 1]).astype(o_ref.dtype)


def _flash_attention(qk, v_ext, key_bias, *, tq, tk):
    lp = qk.shape[0]
    gw = GQA_GROUP * HEAD_DIM
    vw = 2 * HEAD_DIM
    return pl.pallas_call(
        _flash_kernel,
        grid=(N_KV_HEADS, lp // tq, lp // tk),
        in_specs=[pl.BlockSpec((tq, gw), lambda h, i, j: (i, h)),
                  pl.BlockSpec((tk, HEAD_DIM), lambda h, i, j: (j, N_Q_HEADS + h)),
                  pl.BlockSpec((tk, vw), lambda h, i, j: (j, h)),
                  pl.BlockSpec((1, tk), lambda h, i, j: (0, j))],
        out_specs=pl.BlockSpec((tq, gw), lambda h, i, j: (i, h)),
        out_shape=jax.ShapeDtypeStruct((lp, ATTN_WIDTH), bf16),
        scratch_shapes=[pltpu.VMEM((GQA_GROUP, tq, 1), f32),
                        pltpu.VMEM((GQA_GROUP, tq, 1), f32),
                        pltpu.VMEM((GQA_GROUP, tq, vw), f32),
                        pltpu.VMEM((GQA_GROUP, tq, tk), f32),
                        pltpu.VMEM((GQA_GROUP, tq, tk), bf16)],
        compiler_params=_params("parallel", "parallel", "arbitrary"),
        name="flash_gqa",
    )(qk, qk, v_ext, key_bias)


VROWS = 144
STRIP = 64


def _flash_t_kernel(q_ref, k_ref, vt_ref, o_ref, m_scr, acc_scr, s_scr, p_scr, *, n_valid):
    j = pl.program_id(2)
    last = pl.num_programs(2) - 1
    tk, tq = s_scr.shape[1:]

    @pl.when(j == 0)
    def _():
        m_scr[...] = jnp.full(m_scr.shape, -jnp.inf, f32)
        acc_scr[...] = jnp.zeros(acc_scr.shape, f32)

    def scores(g, masked):
        q = q_ref[:, g * HEAD_DIM:(g + 1) * HEAD_DIM]
        s = lax.dot_general(k_ref[...], q, (((1,), (1,)), ((), ())), preferred_element_type=f32)
        if masked:
            key = j * tk + lax.broadcasted_iota(jnp.int32, (tk, 1), 0)
            s = jnp.where(key < n_valid, s, MASK_BIAS)
        s_scr[g] = s
        m_prev = m_scr[g]
        m_new = jnp.maximum(m_prev, jnp.max(s, axis=0, keepdims=True))
        acc_scr[g] = jnp.exp2(m_prev - m_new) * acc_scr[g]
        m_scr[g] = m_new

    def softmax(g):
        m = m_scr[g]
        for r in range(0, tk, STRIP):
            p_scr[g, r:r + STRIP, :] = jnp.exp2(s_scr[g, r:r + STRIP, :] - m).astype(bf16)

    def weighted_sum(g):
        acc_scr[g] += _dot(vt_ref[...], p_scr[g])

    def step(masked):
        scores(0, masked)
        for g in range(GQA_GROUP):
            if g + 1 < GQA_GROUP:
                scores(g + 1, masked)
            softmax(g)
            weighted_sum(g)

    pl.when(j < last)(lambda: step(False))
    pl.when(j == last)(lambda: step(True))

    @pl.when(j == last)
    def _():
        for g in range(GQA_GROUP):
            acc = acc_scr[g]
            out_t = acc[:HEAD_DIM] / acc[HEAD_DIM:HEAD_DIM + 1]
            o_ref[:, g * HEAD_DIM:(g + 1) * HEAD_DIM] = out_t.T.astype(o_ref.dtype)


def flash_attention_t(qk, v_t, n_valid, *, tq, tk):
    lp = qk.shape[0]
    gw = GQA_GROUP * HEAD_DIM
    return pl.pallas_call(
        functools.partial(_flash_t_kernel, n_valid=n_valid),
        grid=(N_KV_HEADS, lp // tq, lp // tk),
        in_specs=[pl.BlockSpec((tq, gw), lambda h, i, j: (i, h)),
                  pl.BlockSpec((tk, HEAD_DIM), lambda h, i, j: (j, N_Q_HEADS + h)),
                  pl.BlockSpec((None, VROWS, tk), lambda h, i, j: (h, 0, j))],
        out_specs=pl.BlockSpec((tq, gw), lambda h, i, j: (i, h)),
        out_shape=jax.ShapeDtypeStruct((lp, ATTN_WIDTH), bf16),
        scratch_shapes=[pltpu.VMEM((GQA_GROUP, 1, tq), f32),
                        pltpu.VMEM((GQA_GROUP, VROWS, tq), f32),
                        pltpu.VMEM((GQA_GROUP, tk, tq), f32),
                        pltpu.VMEM((GQA_GROUP, tk, tq), bf16)],
        compiler_params=_params("parallel", "parallel", "arbitrary"),
        name="flash_gqa_keymajor",
    )(qk, qk, v_t)


def _conv3_rows(x, prev, nxt, w, row0, n_valid):
    tr = x.shape[0]
    loc = lax.broadcasted_iota(jnp.int32, (tr, 1), 0)
    t = loc + row0
    xm = jnp.where(loc == 0, prev[SUBLANES - 1:SUBLANES, :], pltpu.roll(x, 1, 0))
    xm = jnp.where(t == 0, 0.0, xm)
    xp = jnp.where(loc == tr - 1, nxt[0:1, :], pltpu.roll(x, tr - 1, 0))
    xp = jnp.where(t >= n_valid - 1, 0.0, xp)
    return w[0:1, :] * xm + w[1:2, :] * x + w[2:3, :] * xp


def _halo_specs(tr, tc, n_rows, col_blk0):
    rb = tr // SUBLANES
    last = n_rows // SUBLANES - 1
    return [pl.BlockSpec((tr, tc), lambda i, j: (jnp.minimum(i, n_rows // tr - 1), col_blk0 + j)),
            pl.BlockSpec((SUBLANES, tc), lambda i, j: (jnp.clip(i * rb - 1, 0, last), col_blk0 + j)),
            pl.BlockSpec((SUBLANES, tc), lambda i, j: (jnp.minimum((i + 1) * rb, last), col_blk0 + j))]


def _hyena_conv_kernel(x_ref, p_ref, n_ref, w_ref, b_ref, o_ref, *, n_valid):
    tr = x_ref.shape[0]
    row0 = pl.program_id(0) * tr
    y = _conv3_rows(x_ref[...], p_ref[...], n_ref[...], w_ref[...], row0, n_valid) + b_ref[...]
    t = row0 + lax.broadcasted_iota(jnp.int32, (tr, 1), 0)
    o_ref[...] = jnp.where(t < n_valid, y, 0.0)


def _hyena_conv(proj, w, b, col_off, n_valid, out_rows):
    lp = proj.shape[0]
    width = w.shape[1]
    tc = _pick(math.gcd(col_off, width), (1024, 512, 256, 128))
    tr = _pick(math.gcd(lp, out_rows), (512, 256, 128))
    return pl.pallas_call(
        functools.partial(_hyena_conv_kernel, n_valid=n_valid),
        grid=(out_rows // tr, width // tc),
        in_specs=_halo_specs(tr, tc, lp, col_off // tc) + [
            pl.BlockSpec((3, tc), lambda i, j: (0, j)),
            pl.BlockSpec((1, tc), lambda i, j: (0, j))],
        out_specs=pl.BlockSpec((tr, tc), lambda i, j: (i, j)),
        out_shape=jax.ShapeDtypeStruct((out_rows, width), f32),
        compiler_params=_params("parallel", "parallel"),
        name="hyena_short_conv",
    )(proj, proj, proj, w, b.reshape(1, width))


EMB_COLS = LANES
COL_FWD = FILTER_EMB_DIM
COL_BWD = FILTER_EMB_DIM + 1
COL_T = FILTER_EMB_DIM + 2


def _filter_inputs(seq_len, n_fft, n_rows):
    n = jnp.arange(n_rows)
    fwd = n < seq_len
    bwd = (n > n_fft - seq_len) & (n < n_fft)
    lag = jnp.where(fwd, n, jnp.where(bwd, n_fft - n, 0)).astype(f32)[:, None]
    t = lag / (seq_len - 1)
    w = 2.0 * math.pi * lag / seq_len
    f = jnp.linspace(1e-4, FILTER_BANDS - 1, FILTER_BANDS, dtype=f32)[None, :]
    cols = [t, jnp.cos(f * w), -jnp.sin(f * w), fwd.astype(f32)[:, None], bwd.astype(f32)[:, None], t,
            jnp.zeros((n_rows, EMB_COLS - FILTER_EMB_DIM - 3), f32)]
    return jnp.concatenate(cols, axis=-1)


def _filter_kernel(e_ref, w1_ref, b1_ref, w2_ref, b2_ref, w3_ref, b3_ref, w4_ref, fr_ref, ad_ref,
                   o0_ref, o1_ref):
    e = e_ref[...]
    freq = fr_ref[...]

    def layer(x, w_ref, b_ref):
        xh, xl = _split(x)
        wh, wl = _split(w_ref[...])
        return jnp.sin(freq * (_dot3(xh, xl, wh, wl) + b_ref[...]))

    hid = layer(layer(layer(e, w1_ref, b1_ref), w2_ref, b2_ref), w3_ref, b3_ref)
    hid = hid.astype(bf16)
    scale = (e[:, COL_FWD:COL_FWD + 1] + e[:, COL_BWD:COL_BWD + 1]) * jnp.exp(
        -e[:, COL_T:COL_T + 1] * ad_ref[...])
    o0_ref[...] = _dot(hid, w4_ref[0]) * scale
    o1_ref[...] = _dot(hid, w4_ref[1]) * scale


def _filters(e_ext, w1, b1, w2, b2, w3, b3, w4, freq, n_ch, seq_len):
    n_rows = e_ext.shape[0]
    hid = w2.shape[0]
    tr = LANES
    w1p = jnp.zeros((EMB_COLS, hid), f32).at[:FILTER_EMB_DIM].set(w1)
    w4r = w4.reshape(hid, 2, 2, n_ch).transpose(1, 2, 0, 3).astype(bf16)
    abs_delta = jnp.abs(jnp.linspace(MIN_DECAY, MAX_DECAY, n_ch, dtype=f32)).reshape(1, n_ch)
    full = lambda *shape: pl.BlockSpec(shape, lambda i: (0,) * len(shape))
    fwd_blocks = -(-seq_len // tr)
    w4_spec = pl.BlockSpec((2, None, hid, n_ch), lambda i: (0, jnp.where(i < fwd_blocks, 0, 1), 0, 0))
    out_spec = pl.BlockSpec((tr, n_ch), lambda i: (i, 0))
    return pl.pallas_call(
        _filter_kernel,
        grid=(n_rows // tr,),
        in_specs=[pl.BlockSpec((tr, EMB_COLS), lambda i: (i, 0)),
                  full(EMB_COLS, hid), full(1, hid), full(hid, hid), full(1, hid),
                  full(hid, hid), full(1, hid), w4_spec, full(1, hid), full(1, n_ch)],
        out_specs=[out_spec, out_spec],
        out_shape=[jax.ShapeDtypeStruct((n_rows, n_ch), f32)] * 2,
        compiler_params=_params("parallel"),
        name="hyena_filters",
    )(e_ext, w1p, b1.reshape(1, hid), w2, b2.reshape(1, hid), w3, b3.reshape(1, hid), w4r,
      freq.reshape(1, hid), abs_delta)


def _dft_tables(n1, nz):
    n_fft = n1 * LANES
    k1n = (n1 + 1) // 2
    two_pi = 2.0 * math.pi

    def angle(m, period):
        m = jnp.where(m > period // 2, m - period, m)
        return m.astype(f32) * (two_pi / period)

    row = jnp.arange(2 * k1n * SUBLANES, dtype=jnp.int32)[:, None]
    col = jnp.arange(n1 * SUBLANES, dtype=jnp.int32)[None, :]
    k1r, part = row // (2 * SUBLANES), (row // SUBLANES) % 2
    a1 = angle((k1r * (col // SUBLANES)) % n1, n1)
    fwd = jnp.where(row % SUBLANES == col % SUBLANES,
                    jnp.where(part == 0, jnp.cos(a1), -jnp.sin(a1)), 0.0)
    wgt = jnp.where(k1r == 0, 1.0, 2.0).astype(f32) / n_fft
    inv = (fwd * wgt)[:, :nz * SUBLANES].T
    fwd, inv = fwd.astype(bf16), inv.astype(bf16)

    k1 = jnp.arange(k1n, dtype=jnp.int32)
    n2 = jnp.arange(LANES, dtype=jnp.int32)
    k2 = jnp.arange(LANES, dtype=jnp.int32)
    m = (n2[None, None, :] * (k2[None, :, None] * n1 + k1[:, None, None])) % n_fft
    a2 = angle(m, n_fft)
    gr, gi = jnp.cos(a2), -jnp.sin(a2)
    g = jnp.concatenate([jnp.concatenate([gr, -gi], axis=2),
                         jnp.concatenate([gi, gr], axis=2)], axis=1)
    grt, git = jnp.swapaxes(gr, 1, 2), jnp.swapaxes(gi, 1, 2)
    ginv = jnp.concatenate([jnp.concatenate([grt, git], axis=2),
                            jnp.concatenate([-git, grt], axis=2)], axis=1)
    return fwd, g.astype(bf16), ginv.astype(bf16), inv


def _resident(shape):
    return pl.BlockSpec(shape, lambda *_: (0,) * len(shape), pipeline_mode=pl.Buffered(1))


def _dft_rows_kernel(m_ref, x_ref, o_ref):
    kd, sub, tw = x_ref.shape
    x = x_ref[...].reshape(kd * sub, tw).astype(bf16)
    o_ref[...] = _dot(m_ref[...], x).reshape(o_ref.shape)


def _dft_rows(mat8, x3, n_ch, *, slot, tw):
    r, kd = mat8.shape[0] // SUBLANES, mat8.shape[1] // SUBLANES
    ct = n_ch // tw
    return pl.pallas_call(
        _dft_rows_kernel,
        grid=(LANES // SUBLANES, ct),
        in_specs=[_resident(mat8.shape),
                  pl.BlockSpec((kd, SUBLANES, tw), lambda jb, c: (0, jb, slot * ct + c))],
        out_specs=pl.BlockSpec((r, SUBLANES, tw), lambda jb, c: (0, jb, c)),
        out_shape=jax.ShapeDtypeStruct((r, LANES, n_ch), f32),
        compiler_params=_params("parallel", "parallel"),
        name="dft_outer_fwd",
    )(mat8, x3)


def _spectrum_kernel(g_ref, a_ref, o_ref):
    o_ref[...] = _dot(g_ref[...], a_ref[...].astype(bf16))


def _spectrum(g, a3, *, tc):
    k1n, two_l, n_ch = a3.shape
    gspec = pl.BlockSpec((None, two_l, two_l), lambda k, c: (k, 0, 0))
    dspec = pl.BlockSpec((None, two_l, tc), lambda k, c: (k, 0, c))
    return pl.pallas_call(
        _spectrum_kernel,
        grid=(k1n, n_ch // tc),
        in_specs=[gspec, dspec],
        out_specs=dspec,
        out_shape=jax.ShapeDtypeStruct(a3.shape, f32),
        compiler_params=_params("parallel", "parallel"),
        name="dft_inner_fwd",
    )(g, a3)


def _spectral_conv_kernel(g_ref, ginv_ref, a_ref, kf_ref, o_ref):
    x = _dot(g_ref[...], a_ref[...].astype(bf16))
    kf = kf_ref[...]
    xr, xi = x[:LANES], x[LANES:]
    kr, ki = kf[:LANES], kf[LANES:]
    y = jnp.concatenate([xr * kr - xi * ki, xr * ki + xi * kr], axis=0)
    o_ref[...] = _dot(ginv_ref[...], y.astype(bf16))


def _spectral_conv(g, ginv, a3, kf3, *, tc):
    k1n, two_l, n_ch = a3.shape
    gspec = pl.BlockSpec((None, two_l, two_l), lambda k, c: (k, 0, 0))
    dspec = pl.BlockSpec((None, two_l, tc), lambda k, c: (k, 0, c))
    return pl.pallas_call(
        _spectral_conv_kernel,
        grid=(k1n, n_ch // tc),
        in_specs=[gspec, gspec, dspec, dspec],
        out_specs=dspec,
        out_shape=jax.ShapeDtypeStruct(a3.shape, f32),
        compiler_params=_params("parallel", "parallel"),
        name="dft_inner_conv",
    )(g, ginv, a3, kf3)


def _idft_gate_kernel(m_ref, b_ref, z_ref, gate_ref, skip_ref, o_ref):
    r, sub, tw = b_ref.shape
    y = _dot(m_ref[...], b_ref[...].reshape(r * sub, tw).astype(bf16)).reshape(o_ref.shape)
    o_ref[...] = (gate_ref[...] * (y + skip_ref[...] * z_ref[...])).astype(o_ref.dtype)


def _idft_gate(mat8, b3, z3, hyc3, gate_slot, skip, out_dtype, *, tw):
    nz, r = mat8.shape[0] // SUBLANES, mat8.shape[1] // SUBLANES
    n_ch = skip.shape[1]
    ct = n_ch // tw
    tile = lambda c_of: pl.BlockSpec((nz, SUBLANES, tw), lambda jb, c: (0, jb, c_of(c)))
    return pl.pallas_call(
        _idft_gate_kernel,
        grid=(LANES // SUBLANES, ct),
        in_specs=[_resident(mat8.shape),
                  pl.BlockSpec((r, SUBLANES, tw), lambda jb, c: (0, jb, c)),
                  tile(lambda c: c),
                  tile(lambda c: gate_slot * ct + c),
                  pl.BlockSpec((1, tw), lambda jb, c: (0, c))],
        out_specs=tile(lambda c: c),
        out_shape=jax.ShapeDtypeStruct((nz, LANES, n_ch), out_dtype),
        compiler_params=_params("parallel", "parallel"),
        name="dft_outer_inv_gate",
    )(mat8, b3, z3, hyc3, skip)


def kernel(x, meta_tokens, norm_mix, w_in, q_norm, k_norm, hyena_conv_w, hyena_conv_b, filt_w1, filt_b1, filt_w2, filt_b2, filt_w3, filt_b3, filt_w4, filt_freq, hyena_skip, w_attn_branch, w_hyena_branch, w_out, norm_ffn, w_ffn_gate, w_ffn_up, ffn_conv_w, ffn_conv_b, w_ffn_down, norm_final):
    batch, n_tok, d_model = x.shape
    depth = norm_mix.shape[0]
    n_ch = w_hyena_branch.shape[1]
    d_ff = w_ffn_gate.shape[2]
    seq_len = n_tok + N_META_TOKENS
    lp = _round_up(seq_len, ROW_PAD)
    nz = lp // LANES
    n1 = -(-(seq_len - 1 + _round_up(seq_len, LANES)) // LANES) | 1
    n_fft = n1 * LANES
    k1n = (n1 + 1) // 2
    hy_off = ATTN_WIDTH + 2 * KV_WIDTH
    ga_off = hy_off + 3 * n_ch
    gb_off = ga_off + d_model

    cos, sin_up, sin_dn = _rope_tables(n_tok, lp)
    key_bias = jnp.where(jnp.arange(lp) < seq_len, 0.0, MASK_BIAS).astype(f32).reshape(1, lp)
    e_ext = _filter_inputs(seq_len, n_fft, n_fft)
    fwd_m, g_m, ginv_m, inv_m = _dft_tables(n1, nz)
    fwd_z = fwd_m[:, :nz * SUBLANES]

    tm = _pick(lp, (512, 256, 128))
    tq = _pick(lp, (1536, 1024, 512, 256, 128))
    tk = _pick(lp, (512, 256, 128))
    assert lp - tk < seq_len, "every key tile must hold at least one real token"
    tw = _pick(n_ch, (2048, 1024, 512, 256, 128))
    tw_outer = _pick(n_ch, (512, 256, 128))

    outs = []
    for b in range(batch):
        h = jnp.concatenate([meta_tokens.astype(f32), x[b],
                             jnp.zeros((lp - seq_len, d_model), f32)], axis=0)
        for i in range(depth):
            u = _rmsnorm(h, norm_mix[i], seq_len, bf16)
            in_w = w_in[i].shape[1]
            proj = _matmul(u, w_in[i].astype(bf16), f32, tm=tm,
                           tn=_pick(in_w, (1024, 512, 256, 128)), name="in_proj")

            q_gain = q_norm[i] * (HEAD_DIM ** -0.5 * math.log2(math.e))
            gains = jnp.concatenate([jnp.broadcast_to(q_gain, (N_Q_HEADS, HEAD_DIM)),
                                     jnp.broadcast_to(k_norm[i], (N_KV_HEADS, HEAD_DIM))])
            qk = _qk_prep(proj, gains.reshape(-1, 1, HEAD_DIM), cos, sin_up, sin_dn)
            v = proj[:, ATTN_WIDTH + KV_WIDTH:hy_off].astype(bf16).reshape(lp, N_KV_HEADS, HEAD_DIM)
            v_t = jnp.concatenate([v.transpose(1, 2, 0), jnp.ones((N_KV_HEADS, 1, lp), bf16),
                                   jnp.zeros((N_KV_HEADS, VROWS - HEAD_DIM - 1, lp), bf16)], axis=1)
            y_attn = flash_attention_t(qk, v_t, seq_len, tq=tq, tk=tk)

            hyc = _hyena_conv(proj, hyena_conv_w[i], hyena_conv_b[i], hy_off, seq_len, lp)
            hyc3 = hyc.reshape(nz, LANES, 3 * n_ch)
            filt = _filters(e_ext, filt_w1[i], filt_b1[i], filt_w2[i], filt_b2[i], filt_w3[i],
                            filt_b3[i], filt_w4[i], filt_freq[i], n_ch, seq_len)
            z3 = hyc3
            for o in range(2):
                kf_a = _dft_rows(fwd_m, filt[o].reshape(n1, LANES, n_ch), n_ch, slot=0, tw=tw_outer)
                kf3 = _spectrum(g_m, kf_a.reshape(k1n, 2 * LANES, n_ch), tc=tw)
                za = _dft_rows(fwd_z, z3, n_ch, slot=0, tw=tw_outer)
                zb = _spectral_conv(g_m, ginv_m, za.reshape(k1n, 2 * LANES, n_ch), kf3, tc=tw)
                z3 = _idft_gate(inv_m, zb.reshape(2 * k1n, LANES, n_ch), z3, hyc3, 1 + o,
                                hyena_skip[i, o].reshape(1, n_ch), f32, tw=tw_outer)
            z = z3.reshape(lp, n_ch)

            tn_m = _pick(math.gcd(math.gcd(ga_off, gb_off), d_model), (1024, 512, 256, 128))
            mixed = _merge(y_attn, z, w_attn_branch[i].astype(bf16), w_hyena_branch[i].astype(bf16),
                           proj, ga_off, gb_off, tm=tm, tn=tn_m)
            h = _matmul(mixed, w_out[i].astype(bf16), f32, tm=tm,
                        tn=_pick(d_model, (1024, 512, 256, 128)), res=h, name="out_proj")

            u2 = _rmsnorm(h, norm_ffn[i], seq_len, bf16)
            act = _ffn_gate(u2, w_ffn_gate[i].astype(bf16), w_ffn_up[i].astype(bf16),
                            ffn_conv_w[i], ffn_conv_b[i], seq_len,
                            tm=_pick(lp, (1536, 1024, 512, 256, 128)), tn=_pick(d_ff, (256, 128)))
            h = _matmul(act, w_ffn_down[i].astype(bf16), f32, tm=tm,
                        tn=_pick(d_model, (512, 256, 128)), res=h, name="ffn_down")
        outs.append(_final_norm(h, norm_final, n_tok)[None])
    return outs[0] if batch == 1 else jnp.concatenate(outs, axis=0)
```
